```python
import jax
import jax.numpy as jnp
from jax import lax

D_MODEL = 1024
BATCH = 32
SEQ = 2048
DEPTH = 4

HEAD_DIM = 64
N_HEADS_TOTAL = D_MODEL // HEAD_DIM
N_FOX_HEADS = N_HEADS_TOTAL // 2
N_DSA_HEADS = N_HEADS_TOTAL - N_FOX_HEADS
N_IDX_HEADS = 4
IDX_DIM = 64
DSA_TOPK_MAX = 256
ROPE_DIM = HEAD_DIM // 4
ROPE_THETA = 500000.0
Q_BLOCK = 128
DSA_Q_BLOCK = 32
CONV_WIDTH = 31
CONV_DIM = D_MODEL
FFN_DIM = 2816
N_EXPERTS = 8
TOP_K_EXPERTS = 2
EXPERT_DIM = 3584
LN_EPS = 1e-5
DEEPNORM_ALPHA = (2.0 * DEPTH) ** 0.25
DEEPNORM_BETA = (8.0 * DEPTH) ** -0.25
N_EVEN = (DEPTH + 1) // 2
N_ODD = DEPTH // 2
FOX_W = N_FOX_HEADS * HEAD_DIM
DSA_W = N_DSA_HEADS * HEAD_DIM
PROJ_SIZES = (FOX_W, FOX_W, FOX_W, N_FOX_HEADS, DSA_W, DSA_W, DSA_W,
              N_IDX_HEADS * IDX_DIM, IDX_DIM, N_IDX_HEADS)
PROJ_DIM = 3 * FOX_W + N_FOX_HEADS + 3 * DSA_W + N_IDX_HEADS * IDX_DIM + IDX_DIM + N_IDX_HEADS

kernel_name = "hybrid_fox_dsa_conformer_moe_deepnorm"


def _split_points(sizes):
    points, acc = [], 0
    for s in sizes[:-1]:
        acc += s
        points.append(acc)
    return points


def layer_norm(x, g, b):
    xf = x.astype(jnp.float32)
    mu = jnp.mean(xf, axis=-1, keepdims=True)
    var = jnp.mean(jnp.square(xf - mu), axis=-1, keepdims=True)
    y = (xf - mu) * lax.rsqrt(var + LN_EPS)
    return (y * g.astype(jnp.float32) + b.astype(jnp.float32)).astype(x.dtype)


def rope_tables(seq):
    inv_freq = ROPE_THETA ** (-jnp.arange(0, ROPE_DIM, 2, dtype=jnp.float32) / ROPE_DIM)
    ang = jnp.arange(seq, dtype=jnp.float32)[:, None] * inv_freq[None, :]
    return jnp.cos(ang), jnp.sin(ang)


def apply_partial_rope(t, cos, sin):
    half = ROPE_DIM // 2
    c = cos[None, :, None, :].astype(t.dtype)
    s = sin[None, :, None, :].astype(t.dtype)
    x1 = t[..., :half]
    x2 = t[..., half:ROPE_DIM]
    return jnp.concatenate([x1 * c - x2 * s, x2 * c + x1 * s, t[..., ROPE_DIM:]], axis=-1)


def forgetting_attention(q, k, v, log_f):
    B, S, H, d = q.shape
    scale = d ** -0.5
    n_blk = S // Q_BLOCK
    FT = jnp.cumsum(log_f, axis=1).transpose(0, 2, 1)
    q_blocks = q.reshape(B, n_blk, Q_BLOCK, H, d).transpose(1, 0, 2, 3, 4)
    F_blocks = FT.reshape(B, H, n_blk, Q_BLOCK).transpose(2, 0, 1, 3)
    key_pos = jnp.arange(S)

    def block(args):
        i, q_i, F_i = args
        s = jnp.einsum('bqhd,bkhd->bhqk', q_i, k).astype(jnp.float32) * scale
        s = s + F_i[..., None] - FT[:, :, None, :]
        q_pos = i * Q_BLOCK + jnp.arange(Q_BLOCK)
        causal = key_pos[None, :] <= q_pos[:, None]
        s = jnp.where(causal[None, None], s, -jnp.inf)
        p = jax.nn.softmax(s, axis=-1).astype(v.dtype)
        return jnp.einsum('bhqk,bkhd->bqhd', p, v)

    out = lax.map(block, (jnp.arange(n_blk), q_blocks, F_blocks))
    return out.transpose(1, 0, 2, 3, 4).reshape(B, S, H, d)


def dsa_attention(q, k, v, q_idx, k_idx, w_idx):
    B, S, H, d = q.shape
    topk = min(DSA_TOPK_MAX, S // 4)
    scale = d ** -0.5
    n_blk = S // DSA_Q_BLOCK
    w_scaled = w_idx * ((N_IDX_HEADS * IDX_DIM) ** -0.5)
    q_blocks = q.reshape(B, n_blk, DSA_Q_BLOCK, H, d).transpose(1, 0, 2, 3, 4)
    qi_blocks = q_idx.reshape(B, n_blk, DSA_Q_BLOCK, N_IDX_HEADS, IDX_DIM).transpose(1, 0, 2, 3, 4)
    w_blocks = w_scaled.reshape(B, n_blk, DSA_Q_BLOCK, N_IDX_HEADS).transpose(1, 0, 2, 3)
    key_pos = jnp.arange(S)
    gather_rows = jax.vmap(lambda src, idx: src[idx])

    def block(args):
        i, q_i, qi_i, w_i = args
        q_pos = i * DSA_Q_BLOCK + jnp.arange(DSA_Q_BLOCK)
        idx_logits = jnp.einsum('bqhd,bkd->bqhk', qi_i, k_idx)
        score = jnp.einsum('bqh,bqhk->bqk', w_i, jax.nn.relu(idx_logits)).astype(jnp.float32)
        causal = key_pos[None, :] <= q_pos[:, None]
        score = jnp.where(causal[None], score, -jnp.inf)
        _, sel = lax.top_k(score, topk)
        valid = sel <= q_pos[None, :, None]
        k_sel = gather_rows(k, sel)
        v_sel = gather_rows(v, sel)
        s = jnp.einsum('bqhd,bqkhd->bhqk', q_i, k_sel).astype(jnp.float32) * scale
        s = jnp.where(valid[:, None], s, -jnp.inf)
        p = jax.nn.softmax(s, axis=-1).astype(v.dtype)
        return jnp.einsum('bhqk,bqkhd->bqhd', p, v_sel)

    out = lax.map(block, (jnp.arange(n_blk), q_blocks, qi_blocks, w_blocks))
    return out.transpose(1, 0, 2, 3, 4).reshape(B, S, H, d)


def attention_mixer(x, w_in, b_f, w_out, cos, sin):
    B, S, _ = x.shape
    proj = jnp.einsum('bsd,dp->bsp', x, w_in)
    fq, fk, fv, ff, dq, dk, dv, iq, ik, iw = jnp.split(proj, _split_points(PROJ_SIZES), axis=-1)
    log_f = jax.nn.log_sigmoid(ff.astype(jnp.float32) + b_f.astype(jnp.float32))
    out_a = forgetting_attention(fq.reshape(B, S, N_FOX_HEADS, HEAD_DIM),
                                 fk.reshape(B, S, N_FOX_HEADS, HEAD_DIM),
                                 fv.reshape(B, S, N_FOX_HEADS, HEAD_DIM), log_f)
    dq = apply_partial_rope(dq.reshape(B, S, N_DSA_HEADS, HEAD_DIM), cos, sin)
    dk = apply_partial_rope(dk.reshape(B, S, N_DSA_HEADS, HEAD_DIM), cos, sin)
    iq = apply_partial_rope(iq.reshape(B, S, N_IDX_HEADS, IDX_DIM), cos, sin)
    ik = apply_partial_rope(ik[:, :, None, :], cos, sin)[:, :, 0, :]
    out_b = dsa_attention(dq, dk, dv.reshape(B, S, N_DSA_HEADS, HEAD_DIM), iq, ik, iw)
    heads = jnp.concatenate([out_a.reshape(B, S, FOX_W), out_b.reshape(B, S, DSA_W)], axis=-1)
    return jnp.einsum('bsc,cd->bsd', heads, w_out)


def conformer_conv(x, w_in, b_in, w_dw, b_dw, ln_g, ln_b, w_out, b_out):
    h = jnp.einsum('bsd,dc->bsc', x, w_in) + b_in
    a, gate = jnp.split(h, 2, axis=-1)
    h = a * jax.nn.sigmoid(gate)
    h = lax.conv_general_dilated(h, w_dw[:, None, :], window_strides=(1,),
                                 padding=[(CONV_WIDTH - 1, 0)],
                                 dimension_numbers=('NWC', 'WIO', 'NWC'),
                                 feature_group_count=CONV_DIM) + b_dw
    h = jax.nn.silu(layer_norm(h, ln_g, ln_b))
    return jnp.einsum('bsc,cd->bsd', h, w_out) + b_out


def swiglu(x, w_gate, w_up, w_down):
    h = jax.nn.silu(jnp.einsum('bsd,df->bsf', x, w_gate)) * jnp.einsum('bsd,df->bsf', x, w_up)
    return jnp.einsum('bsf,fd->bsd', h, w_down)


def moe_swiglu(x, w_router, w_gate, w_up, w_down):
    B, S, D = x.shape
    xt = x.reshape(B * S, D)
    logits = jnp.einsum('nd,de->ne', xt, w_router).astype(jnp.float32)
    top_val, top_idx = lax.top_k(logits, TOP_K_EXPERTS)
    gates = jax.nn.softmax(top_val, axis=-1)
    combine = jnp.sum(jax.nn.one_hot(top_idx, N_EXPERTS, dtype=jnp.float32) * gates[..., None], axis=1)
    out = jnp.zeros_like(xt)
    for e in range(N_EXPERTS):
        h = jax.nn.silu(xt @ w_gate[e]) * (xt @ w_up[e])
        out = out + combine[:, e:e + 1].astype(x.dtype) * (h @ w_down[e])
    return out.reshape(B, S, D)


def setup_inputs(seed: int = 0) -> dict:
    key = jax.random.key(seed)
    ks = jax.random.split(key, 24)
    f32 = jnp.float32
    D = D_MODEL

    def nrm(k, shape, scale):
        return jax.random.normal(k, shape, f32) * scale

    return {
        'x': nrm(ks[0], (BATCH, SEQ, D), 1.0),
        'norm_mix_g': 1.0 + nrm(ks[1], (DEPTH, D), 0.02),
        'norm_mix_b': nrm(ks[2], (DEPTH, D), 0.02),
        'norm_ffn_g': 1.0 + nrm(ks[3], (DEPTH, D), 0.02),
        'norm_ffn_b': nrm(ks[4], (DEPTH, D), 0.02),
        'attn_w_in': nrm(ks[5], (N_EVEN, D, PROJ_DIM), D ** -0.5),
        'attn_b_f': 3.0 + nrm(ks[6], (N_EVEN, N_FOX_HEADS), 0.5),
        'attn_w_out': nrm(ks[7], (N_EVEN, FOX_W + DSA_W, D), (FOX_W + DSA_W) ** -0.5 * DEEPNORM_BETA),
        'ffn_w_gate': nrm(ks[8], (N_EVEN, D, FFN_DIM), D ** -0.5),
        'ffn_w_up': nrm(ks[9], (N_EVEN, D, FFN_DIM), D ** -0.5),
        'ffn_w_down': nrm(ks[10], (N_EVEN, FFN_DIM, D), FFN_DIM ** -0.5 * DEEPNORM_BETA),
        'conv_w_in': nrm(ks[11], (N_ODD, D, 2 * CONV_DIM), D ** -0.5),
        'conv_b_in': nrm(ks[12], (N_ODD, 2 * CONV_DIM), 0.02),
        'conv_w_dw': nrm(ks[13], (N_ODD, CONV_WIDTH, CONV_DIM), CONV_WIDTH ** -0.5),
        'conv_b_dw': nrm(ks[14], (N_ODD, CONV_DIM), 0.02),
        'conv_ln_g': 1.0 + nrm(ks[15], (N_ODD, CONV_DIM), 0.02),
        'conv_ln_b': nrm(ks[16], (N_ODD, CONV_DIM), 0.02),
        'conv_w_out': nrm(ks[17], (N_ODD, CONV_DIM, D), CONV_DIM ** -0.5 * DEEPNORM_BETA),
        'conv_b_out': nrm(ks[18], (N_ODD, D), 0.02),
        'moe_w_router': nrm(ks[19], (N_ODD, D, N_EXPERTS), D ** -0.5),
        'moe_w_gate': nrm(ks[20], (N_ODD, N_EXPERTS, D, EXPERT_DIM), D ** -0.5),
        'moe_w_up': nrm(ks[21], (N_ODD, N_EXPERTS, D, EXPERT_DIM), D ** -0.5),
        'moe_w_down': nrm(ks[22], (N_ODD, N_EXPERTS, EXPERT_DIM, D), EXPERT_DIM ** -0.5 * DEEPNORM_BETA),
    }


def reference(x, norm_mix_g, norm_mix_b, norm_ffn_g, norm_ffn_b,
              attn_w_in, attn_b_f, attn_w_out,
              ffn_w_gate, ffn_w_up, ffn_w_down,
              conv_w_in, conv_b_in, conv_w_dw, conv_b_dw, conv_ln_g, conv_ln_b, conv_w_out, conv_b_out,
              moe_w_router, moe_w_gate, moe_w_up, moe_w_down):
    cos, sin = rope_tables(x.shape[1])
    for layer in range(DEPTH):
        j = layer // 2
        if layer % 2 == 0:
            mix = attention_mixer(x, attn_w_in[j], attn_b_f[j], attn_w_out[j], cos, sin)
        else:
            mix = conformer_conv(x, conv_w_in[j], conv_b_in[j], conv_w_dw[j], conv_b_dw[j],
                                 conv_ln_g[j], conv_ln_b[j], conv_w_out[j], conv_b_out[j])
        x = layer_norm(DEEPNORM_ALPHA * x + mix, norm_mix_g[layer], norm_mix_b[layer])
        if layer % 2 == 0:
            ff = swiglu(x, ffn_w_gate[j], ffn_w_up[j], ffn_w_down[j])
        else:
            ff = moe_swiglu(x, moe_w_router[j], moe_w_gate[j], moe_w_up[j], moe_w_down[j])
        x = layer_norm(DEEPNORM_ALPHA * x + ff, norm_ffn_g[layer], norm_ffn_b[layer])
    return x
```

```python
import functools

import jax
import jax.numpy as jnp
from jax import lax
from jax.experimental import pallas as pl
from jax.experimental.pallas import tpu as pltpu

F32 = jnp.float32
BF16 = jnp.bfloat16

D_MODEL = 1024
DEPTH = 4
HEAD_DIM = 64
FOX_W = 512
DSA_W = 512
N_FOX_HEADS = 8
N_IDX_HEADS = 4
IDX_DIM = 64
DSA_TOPK_MAX = 256
ROPE_DIM = 16
ROPE_THETA = 500000.0
CONV_WIDTH = 31
N_EXPERTS = 8
LN_EPS = 1e-5
ALPHA = (2.0 * DEPTH) ** 0.25

LANES = 128
NEG = -1e30
INT_MIN = -2 ** 31
KEY_NEG_INF = -2 ** 31 + 0x007FFFFF
VMEM_LIMIT = 56 * 1024 * 1024

ATT_BLOCK = 256
MAIN_COLS = 3328
MAIN_TN = 256
ROPE_TILES = (6, 7, 8, 9, 12)


def _cparams(sem):
    return pltpu.CompilerParams(dimension_semantics=sem, vmem_limit_bytes=VMEM_LIMIT)


def _layer_norm(y, g, b):
    mu = jnp.mean(y, axis=-1, keepdims=True)
    d = y - mu
    var = jnp.mean(d * d, axis=-1, keepdims=True)
    return d * lax.rsqrt(var + LN_EPS) * g + b


def _sigmoid(z):
    return 1.0 / (1.0 + jnp.exp(-z))


def _proj_kernel(x_ref, w_ref, c_ref, s1_ref, s2_ref, o_ref, *, rope_tiles):
    acc = jnp.dot(x_ref[...], w_ref[...], preferred_element_type=F32)
    j = pl.program_id(1)
    is_rope = functools.reduce(jnp.logical_or, [j == t for t in rope_tiles])

    @pl.when(is_rope)
    def _():
        tn = acc.shape[1]
        r = (acc * c_ref[...] + pltpu.roll(acc, tn - 8, 1) * s1_ref[...]
             + pltpu.roll(acc, 8, 1) * s2_ref[...])
        o_ref[...] = r.astype(o_ref.dtype)

    @pl.when(jnp.logical_not(is_rope))
    def _():
        o_ref[...] = acc.astype(o_ref.dtype)


def _project(xb, w, tables, rope_tiles, tn, out_dtype, seq, tm):
    n, d = xb.shape
    cols = w.shape[1]
    pos_blocks = seq // tm
    tab_spec = pl.BlockSpec((tm, tn), lambda i, j: (i % pos_blocks, 0))
    return pl.pallas_call(
        functools.partial(_proj_kernel, rope_tiles=rope_tiles),
        grid=(n // tm, cols // tn),
        in_specs=[pl.BlockSpec((tm, d), lambda i, j: (i, 0)),
                  pl.BlockSpec((d, tn), lambda i, j: (0, j)),
                  tab_spec, tab_spec, tab_spec],
        out_specs=pl.BlockSpec((tm, tn), lambda i, j: (i, j)),
        out_shape=jax.ShapeDtypeStruct((n, cols), out_dtype),
        compiler_params=_cparams(("parallel", "arbitrary")),
    )(xb, w, *tables)


def _forget_kernel(z_ref, b_ref, o_ref, *, chunk):
    s = z_ref.shape[0]
    z = z_ref[...] + b_ref[...]
    logf = jnp.minimum(z, 0.0) - jnp.log(1.0 + jnp.exp(-jnp.abs(z)))
    r = lax.broadcasted_iota(jnp.int32, (chunk, chunk), 0)
    c = lax.broadcasted_iota(jnp.int32, (chunk, chunk), 1)
    tri = jnp.where(c <= r, 1.0, 0.0).astype(F32)
    carry = jnp.zeros((1, LANES), F32)
    for k in range(s // chunk):
        blk = logf[k * chunk:(k + 1) * chunk, :]
        cs = jnp.dot(tri, blk, precision=lax.Precision.HIGHEST,
                     preferred_element_type=F32) + carry
        o_ref[k * chunk:(k + 1) * chunk, :] = cs
        carry = cs[chunk - 1:chunk, :]


def _forget_cumsum(small, bias_row, batch, seq):
    chunk = min(256, seq)
    return pl.pallas_call(
        functools.partial(_forget_kernel, chunk=chunk),
        grid=(batch,),
        in_specs=[pl.BlockSpec((seq, LANES), lambda b: (b, 1)),
                  pl.BlockSpec((1, LANES), lambda b: (0, 0))],
        out_specs=pl.BlockSpec((seq, LANES), lambda b: (b, 0)),
        out_shape=jax.ShapeDtypeStruct((batch * seq, LANES), F32),
        compiler_params=_cparams(("parallel",)),
    )(small, bias_row)


def _flash_pair(q2, load_kv, bias_fn, n_plain, diag_chunk, tq):
    lane = lax.broadcasted_iota(jnp.int32, (tq, LANES), 1)
    first = lane < HEAD_DIM
    qa = (jnp.where(first, q2, jnp.zeros_like(q2)), jnp.where(first, jnp.zeros_like(q2), q2))

    def chunk(c, carry, diag):
        kc, vc = load_kv(c)
        new = []
        for a in range(2):
            m, l, acc = carry[a]
            s = lax.dot_general(qa[a], kc, (((1,), (1,)), ((), ())),
                                preferred_element_type=F32)
            s = bias_fn(a, c, s, diag)
            m_new = jnp.maximum(m, jnp.max(s, axis=1, keepdims=True))
            alpha = jnp.exp(m - m_new)
            p = jnp.exp(s - m_new)
            l = alpha * l + jnp.sum(p, axis=1, keepdims=True)
            acc = alpha * acc + jnp.dot(p.astype(BF16), vc, preferred_element_type=F32)
            new.append((m_new, l, acc))
        return tuple(new)

    init = tuple((jnp.full((tq, 1), NEG, F32), jnp.zeros((tq, 1), F32),
                  jnp.zeros((tq, LANES), F32)) for _ in range(2))
    carry = lax.fori_loop(0, n_plain, lambda c, cr: chunk(c, cr, False), init)
    if diag_chunk is not None:
        carry = chunk(diag_chunk, carry, True)
    o0 = carry[0][2] / carry[0][1]
    o1 = carry[1][2] / carry[1][1]
    return jnp.where(first, o0, o1)


def _fox_kernel(q_ref, k_ref, v_ref, fc_ref, fr_ref, o_ref, *, tq):
    hp = pl.program_id(1)
    i = pl.program_id(2)
    lane = lax.broadcasted_iota(jnp.int32, (tq, LANES), 1)
    fc = fc_ref[...]
    fq = [jnp.sum(jnp.where(lane == hp * 2 + a, fc, 0.0), axis=1, keepdims=True)
          for a in range(2)]
    row = lax.broadcasted_iota(jnp.int32, (tq, tq), 0)
    col = lax.broadcasted_iota(jnp.int32, (tq, tq), 1)

    def load_kv(c):
        start = pl.multiple_of(c * tq, tq)
        return k_ref[pl.ds(start, tq), :], v_ref[pl.ds(start, tq), :]

    def bias_fn(a, c, s, diag):
        fk = fr_ref[0, pl.ds(hp * 2 + a, 1), pl.ds(c, 1), :].reshape(1, tq)
        s = s + fq[a] - fk
        if diag:
            s = jnp.where(col <= row, s, NEG)
        return s

    out = _flash_pair(q_ref[...], load_kv, bias_fn, i, i, tq)
    o_ref[...] = out.astype(o_ref.dtype)


def _fox_attention(main, fcum, frow, batch, seq):
    tq = min(ATT_BLOCK, seq)
    nq = seq // tq
    n = batch * seq
    pairs = FOX_W // LANES
    return pl.pallas_call(
        functools.partial(_fox_kernel, tq=tq),
        grid=(batch, pairs, nq),
        in_specs=[pl.BlockSpec((tq, LANES), lambda b, h, i: (b * nq + i, h)),
                  pl.BlockSpec((seq, LANES), lambda b, h, i: (b, pairs + h)),
                  pl.BlockSpec((seq, LANES), lambda b, h, i: (b, 2 * pairs + h)),
                  pl.BlockSpec((tq, LANES), lambda b, h, i: (b * nq + i, 0)),
                  pl.BlockSpec((1, N_FOX_HEADS, nq, tq), lambda b, h, i: (b, 0, 0, 0))],
        out_specs=pl.BlockSpec((tq, LANES), lambda b, h, i: (b * nq + i, h)),
        out_shape=jax.ShapeDtypeStruct((n, FOX_W), BF16),
        compiler_params=_cparams(("parallel", "parallel", "arbitrary")),
    )(main, main, main, fcum, frow)


def _dsa_kernel(q_ref, k_ref, v_ref, iq_ref, ik_ref, iw_ref, o_ref,
                key_scr, bias_scr, kidx_scr, *, tq, topk):
    i = pl.program_id(1)
    nch = i + 1

    @pl.when(i == 0)
    def _():
        kidx_scr[...] = ik_ref[...].astype(BF16)

    lane = lax.broadcasted_iota(jnp.int32, (tq, LANES), 1)
    first = lane < IDX_DIM
    row = lax.broadcasted_iota(jnp.int32, (tq, tq), 0)
    col = lax.broadcasted_iota(jnp.int32, (tq, tq), 1)
    iq = iq_ref[...]
    w = iw_ref[...] * ((N_IDX_HEADS * IDX_DIM) ** -0.5)
    qi, wh = [], []
    for h in range(N_IDX_HEADS):
        grp = iq[:, (h // 2) * LANES:(h // 2 + 1) * LANES]
        zero = jnp.zeros_like(grp)
        qi.append(jnp.where(first, grp, zero) if h % 2 == 0 else jnp.where(first, zero, grp))
        wh.append(w[:, N_FOX_HEADS + h:N_FOX_HEADS + h + 1])

    def score_chunk(c, diag):
        start = pl.multiple_of(c * tq, tq)
        kc = kidx_scr[pl.ds(start, tq), :]
        sc = jnp.zeros((tq, tq), F32)
        for h in range(N_IDX_HEADS):
            lg = lax.dot_general(qi[h], kc, (((1,), (1,)), ((), ())),
                                 preferred_element_type=F32)
            sc = sc + wh[h] * jnp.maximum(lg, 0.0)
        if diag:
            sc = jnp.where(col <= row, sc, -jnp.inf)
        sc = jnp.where(sc == 0.0, 0.0, sc)
        bits = pltpu.bitcast(sc, jnp.int32)
        key_scr[c] = bits ^ ((bits >> 31) & 0x7FFFFFFF)

    def plain_body(c, carry):
        score_chunk(c, False)
        return carry

    lax.fori_loop(0, i, plain_body, 0)
    score_chunk(i, True)

    kf = float(topk)

    def count(pred):
        def body(c, part):
            ones = jnp.where(pred(key_scr[c]), 1.0, 0.0)
            for g in range(tq // LANES):
                part = part + ones[:, g * LANES:(g + 1) * LANES]
            return part
        part = lax.fori_loop(0, nch, body, jnp.zeros((tq, LANES), F32))
        return jnp.sum(part, axis=1, keepdims=True)

    def refine(cand, thr):
        return jnp.where(count(lambda k: k >= cand) >= kf, cand, thr)

    thr = jnp.full((tq, 1), INT_MIN, jnp.int32)
    thr = refine(jnp.zeros((tq, 1), jnp.int32), thr)
    thr = lax.fori_loop(
        0, 31, lambda it, t: refine(t | jnp.left_shift(jnp.int32(1), 30 - it), t), thr)

    need = kf - count(lambda k: k > thr)
    need = jnp.where(thr == KEY_NEG_INF, 0.0, need)
    upper = jnp.where(row < col, 1.0, 0.0).astype(BF16)

    def sel_body(c, off):
        k = key_scr[c]
        eq = k == thr
        eqf = jnp.where(eq, 1.0, 0.0)
        before = jnp.dot(eqf.astype(BF16), upper, preferred_element_type=F32) + off
        tie = jnp.where(before < need, 0.0, NEG)
        bias_scr[c] = jnp.where(k > thr, 0.0, jnp.where(eq, tie, NEG))
        return off + jnp.sum(eqf, axis=1, keepdims=True)

    lax.fori_loop(0, nch, sel_body, jnp.zeros((tq, 1), F32))

    for g in range(DSA_W // LANES):
        cols = slice(g * LANES, (g + 1) * LANES)

        def load_kv(c, cols=cols):
            start = pl.multiple_of(c * tq, tq)
            return k_ref[pl.ds(start, tq), cols], v_ref[pl.ds(start, tq), cols]

        def bias_fn(a, c, s, diag):
            return s + bias_scr[c]

        out = _flash_pair(q_ref[:, cols], load_kv, bias_fn, nch, None, tq)
        o_ref[:, cols] = out.astype(o_ref.dtype)


def _dsa_attention(main, small, batch, seq):
    tq = min(ATT_BLOCK, seq)
    nq = seq // tq
    n = batch * seq
    topk = min(DSA_TOPK_MAX, seq // 4)
    iq_block = MAIN_COLS // 256 - 1
    return pl.pallas_call(
        functools.partial(_dsa_kernel, tq=tq, topk=topk),
        grid=(batch, nq),
        in_specs=[pl.BlockSpec((tq, DSA_W), lambda b, i: (b * nq + i, 3)),
                  pl.BlockSpec((seq, DSA_W), lambda b, i: (b, 4)),
                  pl.BlockSpec((seq, DSA_W), lambda b, i: (b, 5)),
                  pl.BlockSpec((tq, 256), lambda b, i: (b * nq + i, iq_block)),
                  pl.BlockSpec((seq, LANES), lambda b, i: (b, 0)),
                  pl.BlockSpec((tq, LANES), lambda b, i: (b * nq + i, 1))],
        out_specs=pl.BlockSpec((tq, DSA_W), lambda b, i: (b * nq + i, 0)),
        out_shape=jax.ShapeDtypeStruct((n, DSA_W), BF16),
        scratch_shapes=[pltpu.VMEM((nq, tq, tq), jnp.int32),
                        pltpu.VMEM((nq, tq, tq), F32),
                        pltpu.VMEM((seq, LANES), BF16)],
        compiler_params=_cparams(("parallel", "arbitrary")),
    )(main, main, main, main, small, small)


def _matmul_ln_kernel(*refs, n_pairs):
    a_refs = refs[:n_pairs]
    w_refs = refs[n_pairs:2 * n_pairs]
    bias_ref, x_ref, g_ref, b_ref, o_ref, ob_ref = refs[2 * n_pairs:]
    acc = bias_ref[...] + ALPHA * x_ref[...]
    for a_ref, w_ref in zip(a_refs, w_refs):
        acc = acc + jnp.dot(a_ref[...], w_ref[...], preferred_element_type=F32)
    y = _layer_norm(acc, g_ref[...], b_ref[...])
    o_ref[...] = y
    ob_ref[...] = y.astype(BF16)


def _matmul_ln(a_list, w_list, bias, x, g, b, tm):
    n, d = x.shape
    n_pairs = len(a_list)
    row = lambda i: (i, 0)
    fixed = lambda i: (0, 0)
    in_specs = ([pl.BlockSpec((tm, a.shape[1]), row) for a in a_list]
                + [pl.BlockSpec(w.shape, fixed) for w in w_list]
                + [pl.BlockSpec((1, d), fixed), pl.BlockSpec((tm, d), row),
                   pl.BlockSpec((1, d), fixed), pl.BlockSpec((1, d), fixed)])
    return pl.pallas_call(
        functools.partial(_matmul_ln_kernel, n_pairs=n_pairs),
        grid=(n // tm,),
        in_specs=in_specs,
        out_specs=[pl.BlockSpec((tm, d), row), pl.BlockSpec((tm, d), row)],
        out_shape=[jax.ShapeDtypeStruct((n, d), F32), jax.ShapeDtypeStruct((n, d), BF16)],
        compiler_params=_cparams(("parallel",)),
    )(*a_list, *w_list, bias, x, g, b)


def _swiglu_kernel(xb_ref, wg_ref, wu_ref, wd_ref, x_ref, g_ref, b_ref, o_ref, ob_ref, acc_ref):
    f = pl.program_id(1)

    @pl.when(f == 0)
    def _():
        acc_ref[...] = ALPHA * x_ref[...]

    xb = xb_ref[...]
    hg = jnp.dot(xb, wg_ref[...], preferred_element_type=F32)
    hu = jnp.dot(xb, wu_ref[...], preferred_element_type=F32)
    h = hg * _sigmoid(hg) * hu
    acc_ref[...] += jnp.dot(h.astype(BF16), wd_ref[...], preferred_element_type=F32)

    @pl.when(f == pl.num_programs(1) - 1)
    def _():
        y = _layer_norm(acc_ref[...], g_ref[...], b_ref[...])
        o_ref[...] = y
        ob_ref[...] = y.astype(BF16)


def _swiglu_ln(xb, x, wg, wu, wd, g, b, tm, tf):
    n, d = x.shape
    ffn = wg.shape[1]
    row = lambda i, f: (i, 0)
    fixed = lambda i, f: (0, 0)
    return pl.pallas_call(
        _swiglu_kernel,
        grid=(n // tm, ffn // tf),
        in_specs=[pl.BlockSpec((tm, d), row),
                  pl.BlockSpec((d, tf), lambda i, f: (0, f)),
                  pl.BlockSpec((d, tf), lambda i, f: (0, f)),
                  pl.BlockSpec((tf, d), lambda i, f: (f, 0)),
                  pl.BlockSpec((tm, d), row),
                  pl.BlockSpec((1, d), fixed), pl.BlockSpec((1, d), fixed)],
        out_specs=[pl.BlockSpec((tm, d), row), pl.BlockSpec((tm, d), row)],
        out_shape=[jax.ShapeDtypeStruct((n, d), F32), jax.ShapeDtypeStruct((n, d), BF16)],
        scratch_shapes=[pltpu.VMEM((tm, d), F32)],
        compiler_params=_cparams(("parallel", "arbitrary")),
    )(xb, wg, wu, wd, x, g, b)


def _glu_kernel(x_ref, wa_ref, wg_ref, ba_ref, bg_ref, o_ref):
    xb = x_ref[...]
    a = jnp.dot(xb, wa_ref[...], preferred_element_type=F32) + ba_ref[...]
    gate = jnp.dot(xb, wg_ref[...], preferred_element_type=F32) + bg_ref[...]
    o_ref[...] = a * _sigmoid(gate)


def _glu_proj(xb, w, bias, tm, tn):
    n, d = xb.shape
    c = w.shape[1] // 2
    nc = c // tn
    return pl.pallas_call(
        _glu_kernel,
        grid=(n // tm, nc),
        in_specs=[pl.BlockSpec((tm, d), lambda i, j: (i, 0)),
                  pl.BlockSpec((d, tn), lambda i, j: (0, j)),
                  pl.BlockSpec((d, tn), lambda i, j: (0, nc + j)),
                  pl.BlockSpec((1, tn), lambda i, j: (0, j)),
                  pl.BlockSpec((1, tn), lambda i, j: (0, nc + j))],
        out_specs=pl.BlockSpec((tm, tn), lambda i, j: (i, j)),
        out_shape=jax.ShapeDtypeStruct((n, c), F32),
        compiler_params=_cparams(("parallel", "arbitrary")),
    )(xb, w, w, bias, bias)


def _dwconv_kernel(cur_ref, prev_ref, w_ref, bdw_ref, g_ref, b_ref, o_ref, buf_ref, y_ref,
                   *, ts, halo):
    i = pl.program_id(1)
    prev = prev_ref[...]
    buf_ref[0:halo, :] = jnp.where(i == 0, jnp.zeros_like(prev), prev)
    buf_ref[halo:halo + ts, :] = cur_ref[...]
    base = halo - (CONV_WIDTH - 1)
    for cb in range(cur_ref.shape[1] // LANES):
        cols = slice(cb * LANES, (cb + 1) * LANES)
        acc = jnp.zeros((ts, LANES), F32)
        for j in range(CONV_WIDTH):
            acc = acc + w_ref[j:j + 1, cols] * buf_ref[base + j:base + j + ts, cols]
        y_ref[:, cols] = acc
    y = _layer_norm(y_ref[...] + bdw_ref[...], g_ref[...], b_ref[...])
    o_ref[...] = (y * _sigmoid(y)).astype(o_ref.dtype)


def _dwconv_ln_silu(h, w_dw, b_dw, g, b, batch, seq, ts):
    n, c = h.shape
    halo = 32
    nt = seq // ts
    ratio = ts // halo
    fixed = lambda bi, i: (0, 0)
    cur = lambda bi, i: (bi * nt + i, 0)
    prev = lambda bi, i: (jnp.maximum((bi * nt + i) * ratio - 1, 0), 0)
    return pl.pallas_call(
        functools.partial(_dwconv_kernel, ts=ts, halo=halo),
        grid=(batch, nt),
        in_specs=[pl.BlockSpec((ts, c), cur), pl.BlockSpec((halo, c), prev),
                  pl.BlockSpec((CONV_WIDTH, c), fixed),
                  pl.BlockSpec((1, c), fixed), pl.BlockSpec((1, c), fixed),
                  pl.BlockSpec((1, c), fixed)],
        out_specs=pl.BlockSpec((ts, c), cur),
        out_shape=jax.ShapeDtypeStruct((n, c), BF16),
        scratch_shapes=[pltpu.VMEM((halo + ts, c), F32), pltpu.VMEM((ts, c), F32)],
        compiler_params=_cparams(("parallel", "arbitrary")),
    )(h, h, w_dw, b_dw, g, b)


def _router_kernel(x_ref, w_ref, o_ref):
    logits = jnp.dot(x_ref[...], w_ref[...], precision=lax.Precision.HIGHEST,
                     preferred_element_type=F32)
    lane = lax.broadcasted_iota(jnp.int32, logits.shape, 1)
    logits = jnp.where(lane < N_EXPERTS, logits, -jnp.inf)
    m1 = jnp.max(logits, axis=1, keepdims=True)
    i1 = jnp.min(jnp.where(logits == m1, lane, LANES), axis=1, keepdims=True)
    rest = jnp.where(lane == i1, -jnp.inf, logits)
    m2 = jnp.max(rest, axis=1, keepdims=True)
    i2 = jnp.min(jnp.where(rest == m2, lane, LANES), axis=1, keepdims=True)
    e2 = jnp.exp(m2 - m1)
    g1 = 1.0 / (1.0 + e2)
    g2 = e2 / (1.0 + e2)
    o_ref[...] = jnp.where(lane == i1, g1, 0.0) + jnp.where(lane == i2, g2, 0.0)


def _router(x, w_router_padded, tm):
    n, d = x.shape
    return pl.pallas_call(
        _router_kernel,
        grid=(n // tm,),
        in_specs=[pl.BlockSpec((tm, d), lambda i: (i, 0)),
                  pl.BlockSpec((d, LANES), lambda i: (0, 0))],
        out_specs=pl.BlockSpec((tm, LANES), lambda i: (i, 0)),
        out_shape=jax.ShapeDtypeStruct((n, LANES), F32),
        compiler_params=_cparams(("parallel",)),
    )(x, w_router_padded)


def _moe_kernel(xb_ref, comb_ref, wg_ref, wu_ref, wd_ref, x_ref, g_ref, b_ref, o_ref, ob_ref,
                acc_ref):
    e = pl.program_id(1)
    f = pl.program_id(2)

    @pl.when(jnp.logical_and(e == 0, f == 0))
    def _():
        acc_ref[...] = ALPHA * x_ref[...]

    comb = comb_ref[...]
    lane = lax.broadcasted_iota(jnp.int32, comb.shape, 1)
    ce = jnp.sum(jnp.where(lane == e, comb, 0.0), axis=1, keepdims=True)
    xb = xb_ref[...]
    hg = jnp.dot(xb, wg_ref[...], preferred_element_type=F32)
    hu = jnp.dot(xb, wu_ref[...], preferred_element_type=F32)
    h = hg * _sigmoid(hg) * hu * ce
    acc_ref[...] += jnp.dot(h.astype(BF16), wd_ref[...], preferred_element_type=F32)

    @pl.when(jnp.logical_and(e == pl.num_programs(1) - 1, f == pl.num_programs(2) - 1))
    def _():
        y = _layer_norm(acc_ref[...], g_ref[...], b_ref[...])
        o_ref[...] = y
        ob_ref[...] = y.astype(BF16)


def _moe_ln(xb, x, comb, wg, wu, wd, g, b, tm, tf):
    n, d = x.shape
    n_exp, _, edim = wg.shape
    row = lambda i, e, f: (i, 0)
    fixed = lambda i, e, f: (0, 0)
    return pl.pallas_call(
        _moe_kernel,
        grid=(n // tm, n_exp, edim // tf),
        in_specs=[pl.BlockSpec((tm, d), row),
                  pl.BlockSpec((tm, LANES), row),
                  pl.BlockSpec((None, d, tf), lambda i, e, f: (e, 0, f)),
                  pl.BlockSpec((None, d, tf), lambda i, e, f: (e, 0, f)),
                  pl.BlockSpec((None, tf, d), lambda i, e, f: (e, f, 0)),
                  pl.BlockSpec((tm, d), row),
                  pl.BlockSpec((1, d), fixed), pl.BlockSpec((1, d), fixed)],
        out_specs=[pl.BlockSpec((tm, d), row), pl.BlockSpec((tm, d), row)],
        out_shape=[jax.ShapeDtypeStruct((n, d), F32), jax.ShapeDtypeStruct((n, d), BF16)],
        scratch_shapes=[pltpu.VMEM((tm, d), F32)],
        compiler_params=_cparams(("parallel", "arbitrary", "arbitrary")),
    )(xb, comb, wg, wu, wd, x, g, b)


def _rope_patterns(seq):
    half = ROPE_DIM // 2
    inv_freq = ROPE_THETA ** (-jnp.arange(0, ROPE_DIM, 2, dtype=F32) / ROPE_DIM)
    ang = jnp.arange(seq, dtype=F32)[:, None] * inv_freq[None, :]
    cos, sin = jnp.cos(ang), jnp.sin(ang)
    rest = HEAD_DIM - ROPE_DIM
    c = jnp.concatenate([cos, cos, jnp.ones((seq, rest), F32)], axis=1)
    s1 = jnp.concatenate([-sin, jnp.zeros((seq, half + rest), F32)], axis=1)
    s2 = jnp.concatenate([jnp.zeros((seq, half), F32), sin, jnp.zeros((seq, rest), F32)], axis=1)
    return c, s1, s2


def _row_tile(n, want):
    return want if n % want == 0 else n


def _attention_layer(x, xb, w_in, b_f, w_out, g, b, tabs_main, tabs_small, batch, seq):
    n = batch * seq
    o = [0, 512, 1024, 1536, 1544, 2056, 2568, 3080, 3336, 3400, 3404]
    part = lambda k: w_in[:, o[k]:o[k + 1]]
    scale = HEAD_DIM ** -0.5
    w_main = jnp.concatenate([part(0) * scale, part(1), part(2), part(4) * scale, part(5),
                              part(6), part(7)], axis=1).astype(BF16)
    pad = jnp.zeros((D_MODEL, LANES - N_FOX_HEADS - N_IDX_HEADS), F32)
    w_small = jnp.concatenate([part(8), part(8), part(3), part(9), pad], axis=1).astype(BF16)
    tm = _row_tile(seq, 512)
    main = _project(xb, w_main, tabs_main, ROPE_TILES, MAIN_TN, BF16, seq, tm)
    small = _project(xb, w_small, tabs_small, (0,), LANES, F32, seq, tm)
    bias_row = jnp.concatenate([b_f.astype(F32), jnp.zeros((LANES - N_FOX_HEADS,), F32)])[None, :]
    fcum = _forget_cumsum(small, bias_row, batch, seq)
    tq = min(ATT_BLOCK, seq)
    frow = jnp.transpose(fcum.reshape(batch, seq, LANES)[:, :, :N_FOX_HEADS], (0, 2, 1))
    frow = frow.reshape(batch, N_FOX_HEADS, seq // tq, tq)
    fox = _fox_attention(main, fcum, frow, batch, seq)
    dsa = _dsa_attention(main, small, batch, seq)
    wo = w_out.astype(BF16)
    zero_bias = jnp.zeros((1, D_MODEL), F32)
    return _matmul_ln([fox, dsa], [wo[:FOX_W], wo[FOX_W:]], zero_bias, x, g, b,
                      _row_tile(n, 512))


def kernel(x, norm_mix_g, norm_mix_b, norm_ffn_g, norm_ffn_b, attn_w_in, attn_b_f, attn_w_out,
           ffn_w_gate, ffn_w_up, ffn_w_down, conv_w_in, conv_b_in, conv_w_dw, conv_b_dw,
           conv_ln_g, conv_ln_b, conv_w_out, conv_b_out, moe_w_router, moe_w_gate, moe_w_up,
           moe_w_down):
    batch, seq, d = x.shape
    n = batch * seq
    pats = _rope_patterns(seq)
    tabs_main = tuple(jnp.tile(p, (1, MAIN_TN // HEAD_DIM)) for p in pats)
    tabs_small = tuple(jnp.tile(p, (1, LANES // HEAD_DIM)) for p in pats)
    r2 = lambda v: v.astype(F32).reshape(1, -1)

    xf = x.reshape(n, d)
    xb = xf.astype(BF16)
    for layer in range(DEPTH):
        j = layer // 2
        gm, bm = r2(norm_mix_g[layer]), r2(norm_mix_b[layer])
        gf, bf = r2(norm_ffn_g[layer]), r2(norm_ffn_b[layer])
        if layer % 2 == 0:
            xf, xb = _attention_layer(xf, xb, attn_w_in[j], attn_b_f[j], attn_w_out[j], gm, bm,
                                      tabs_main, tabs_small, batch, seq)
            xf, xb = _swiglu_ln(xb, xf, ffn_w_gate[j].astype(BF16), ffn_w_up[j].astype(BF16),
                                ffn_w_down[j].astype(BF16), gf, bf, _row_tile(n, 1024), 256)
        else:
            h = _glu_proj(xb, conv_w_in[j].astype(BF16), r2(conv_b_in[j]), _row_tile(n, 512), 256)
            hb = _dwconv_ln_silu(h, conv_w_dw[j].astype(F32), r2(conv_b_dw[j]),
                                 r2(conv_ln_g[j]), r2(conv_ln_b[j]), batch, seq,
                                 min(256, seq))
            xf, xb = _matmul_ln([hb], [conv_w_out[j].astype(BF16)], r2(conv_b_out[j]), xf, gm, bm,
                                _row_tile(n, 512))
            w_r = jnp.concatenate([moe_w_router[j].astype(F32),
                                   jnp.zeros((d, LANES - N_EXPERTS), F32)], axis=1)
            comb = _router(xf, w_r, _row_tile(n, 512))
            xf, xb = _moe_ln(xb, xf, comb, moe_w_gate[j].astype(BF16), moe_w_up[j].astype(BF16),
                             moe_w_down[j].astype(BF16), gf, bf, _row_tile(n, 1024), 512)
    return xf.reshape(batch, seq, d)
```

```python
import functools

import jax
import jax.numpy as jnp
from jax import lax
from jax.experimental import pallas as pl
from jax.experimental.pallas import tpu as pltpu

F32 = jnp.float32
BF16 = jnp.bfloat16

D_MODEL = 1024
DEPTH = 4
HEAD_DIM = 64
FOX_W = 512
DSA_W = 512
N_FOX_HEADS = 8
N_IDX_HEADS = 4
IDX_DIM = 64
DSA_TOPK_MAX = 256
ROPE_DIM = 16
ROPE_THETA = 500000.0
CONV_WIDTH = 31
N_EXPERTS = 8
LN_EPS = 1e-5
ALPHA = (2.0 * DEPTH) ** 0.25

LANES = 128
NEG = -1e30
INT_MIN = -2 ** 31
KEY_NEG_INF = -2 ** 31 + 0x007FFFFF
VMEM_LIMIT = 56 * 1024 * 1024

ATT_BLOCK = 512
MAIN_COLS = 3328
MAIN_TN = 256
ROPE_TILES = (6, 7, 8, 9, 12)


def _cparams(sem):
    return pltpu.CompilerParams(dimension_semantics=sem, vmem_limit_bytes=VMEM_LIMIT)


def _layer_norm(y, g, b):
    mu = jnp.mean(y, axis=-1, keepdims=True)
    d = y - mu
    var = jnp.mean(d * d, axis=-1, keepdims=True)
    return d * lax.rsqrt(var + LN_EPS) * g + b


def _sigmoid(z):
    return 1.0 / (1.0 + jnp.exp(-z))


def _proj_kernel(x_ref, w_ref, c_ref, s1_ref, s2_ref, o_ref, *, rope_tiles):
    acc = jnp.dot(x_ref[...], w_ref[...], preferred_element_type=F32)
    j = pl.program_id(1)
    is_rope = functools.reduce(jnp.logical_or, [j == t for t in rope_tiles])

    @pl.when(is_rope)
    def _():
        tn = acc.shape[1]
        r = (acc * c_ref[...] + pltpu.roll(acc, tn - 8, 1) * s1_ref[...]
             + pltpu.roll(acc, 8, 1) * s2_ref[...])
        o_ref[...] = r.astype(o_ref.dtype)

    @pl.when(jnp.logical_not(is_rope))
    def _():
        o_ref[...] = acc.astype(o_ref.dtype)


def _project(xb, w, tables, rope_tiles, tn, out_dtype, seq, tm):
    n, d = xb.shape
    cols = w.shape[1]
    pos_blocks = seq // tm
    tab_spec = pl.BlockSpec((tm, tn), lambda i, j: (i % pos_blocks, 0))
    return pl.pallas_call(
        functools.partial(_proj_kernel, rope_tiles=rope_tiles),
        grid=(n // tm, cols // tn),
        in_specs=[pl.BlockSpec((tm, d), lambda i, j: (i, 0)),
                  pl.BlockSpec((d, tn), lambda i, j: (0, j)),
                  tab_spec, tab_spec, tab_spec],
        out_specs=pl.BlockSpec((tm, tn), lambda i, j: (i, j)),
        out_shape=jax.ShapeDtypeStruct((n, cols), out_dtype),
        compiler_params=_cparams(("parallel", "arbitrary")),
    )(xb, w, *tables)


def _forget_kernel(z_ref, b_ref, o_ref, *, chunk):
    s = z_ref.shape[0]
    z = z_ref[...] + b_ref[...]
    logf = jnp.minimum(z, 0.0) - jnp.log(1.0 + jnp.exp(-jnp.abs(z)))
    r = lax.broadcasted_iota(jnp.int32, (chunk, chunk), 0)
    c = lax.broadcasted_iota(jnp.int32, (chunk, chunk), 1)
    tri = jnp.where(c <= r, 1.0, 0.0).astype(F32)
    carry = jnp.zeros((1, LANES), F32)
    for k in range(s // chunk):
        blk = logf[k * chunk:(k + 1) * chunk, :]
        cs = jnp.dot(tri, blk, precision=lax.Precision.HIGHEST,
                     preferred_element_type=F32) + carry
        o_ref[k * chunk:(k + 1) * chunk, :] = cs
        carry = cs[chunk - 1:chunk, :]


def _forget_cumsum(small, bias_row, batch, seq):
    chunk = min(256, seq)
    return pl.pallas_call(
        functools.partial(_forget_kernel, chunk=chunk),
        grid=(batch,),
        in_specs=[pl.BlockSpec((seq, LANES), lambda b: (b, 1)),
                  pl.BlockSpec((1, LANES), lambda b: (0, 0))],
        out_specs=pl.BlockSpec((seq, LANES), lambda b: (b, 0)),
        out_shape=jax.ShapeDtypeStruct((batch * seq, LANES), F32),
        compiler_params=_cparams(("parallel",)),
    )(small, bias_row)


def _flash_pair(q2, load_k, load_v, bias_fn, n_plain, diag_chunk, s_scr, tq):
    lane = lax.broadcasted_iota(jnp.int32, (tq, LANES), 1)
    first = lane < HEAD_DIM
    qa = (jnp.where(first, q2, jnp.zeros_like(q2)), jnp.where(first, jnp.zeros_like(q2), q2))
    folds = s_scr.shape[3] // LANES

    def score(c, mx, diag):
        kc = load_k(c)
        new = []
        for a in range(2):
            s = lax.dot_general(qa[a], kc, (((1,), (1,)), ((), ())),
                                preferred_element_type=F32)
            s = bias_fn(a, c, s, diag)
            s_scr[a, c] = s
            m = mx[a]
            for f in range(folds):
                m = jnp.maximum(m, s[:, f * LANES:(f + 1) * LANES])
            new.append(m)
        return tuple(new)

    mx = tuple(jnp.full((tq, LANES), NEG, F32) for _ in range(2))
    mx = lax.fori_loop(0, n_plain, lambda c, m: score(c, m, False), mx)
    n_all = n_plain
    if diag_chunk is not None:
        mx = score(diag_chunk, mx, True)
        n_all = n_plain + 1
    row_max = [jnp.max(m, axis=1, keepdims=True) for m in mx]

    def weigh(c, carry):
        vc = load_v(c)
        new = []
        for a in range(2):
            l, acc = carry[a]
            p = jnp.exp(s_scr[a, c] - row_max[a])
            for f in range(folds):
                l = l + p[:, f * LANES:(f + 1) * LANES]
            acc = acc + jnp.dot(p.astype(BF16), vc, preferred_element_type=F32)
            new.append((l, acc))
        return tuple(new)

    init = tuple((jnp.zeros((tq, LANES), F32), jnp.zeros((tq, LANES), F32)) for _ in range(2))
    carry = lax.fori_loop(0, n_all, weigh, init)
    o0 = carry[0][1] / jnp.sum(carry[0][0], axis=1, keepdims=True)
    o1 = carry[1][1] / jnp.sum(carry[1][0], axis=1, keepdims=True)
    return jnp.where(first, o0, o1)


def _fox_kernel(q_ref, k_ref, v_ref, fc_ref, fr_ref, o_ref, s_scr, *, tq):
    hp = pl.program_id(1)
    i = pl.program_id(2)
    lane = lax.broadcasted_iota(jnp.int32, (tq, LANES), 1)
    fc = fc_ref[...]
    fq = [jnp.sum(jnp.where(lane == hp * 2 + a, fc, 0.0), axis=1, keepdims=True)
          for a in range(2)]
    row = lax.broadcasted_iota(jnp.int32, (tq, tq), 0)
    col = lax.broadcasted_iota(jnp.int32, (tq, tq), 1)

    def load_k(c):
        return k_ref[pl.ds(pl.multiple_of(c * tq, tq), tq), :]

    def load_v(c):
        return v_ref[pl.ds(pl.multiple_of(c * tq, tq), tq), :]

    def bias_fn(a, c, s, diag):
        fk = fr_ref[0, pl.ds(hp * 2 + a, 1), pl.ds(c, 1), :].reshape(1, tq)
        s = s + fq[a] - fk
        if diag:
            s = jnp.where(col <= row, s, NEG)
        return s

    out = _flash_pair(q_ref[...], load_k, load_v, bias_fn, i, i, s_scr, tq)
    o_ref[...] = out.astype(o_ref.dtype)


def _fox_attention(main, fcum, frow, batch, seq):
    tq = min(ATT_BLOCK, seq)
    nq = seq // tq
    n = batch * seq
    pairs = FOX_W // LANES
    return pl.pallas_call(
        functools.partial(_fox_kernel, tq=tq),
        grid=(batch, pairs, nq),
        in_specs=[pl.BlockSpec((tq, LANES), lambda b, h, i: (b * nq + i, h)),
                  pl.BlockSpec((seq, LANES), lambda b, h, i: (b, pairs + h)),
                  pl.BlockSpec((seq, LANES), lambda b, h, i: (b, 2 * pairs + h)),
                  pl.BlockSpec((tq, LANES), lambda b, h, i: (b * nq + i, 0)),
                  pl.BlockSpec((1, N_FOX_HEADS, nq, tq), lambda b, h, i: (b, 0, 0, 0))],
        out_specs=pl.BlockSpec((tq, LANES), lambda b, h, i: (b * nq + i, h)),
        out_shape=jax.ShapeDtypeStruct((n, FOX_W), BF16),
        scratch_shapes=[pltpu.VMEM((2, nq, tq, tq), F32)],
        compiler_params=_cparams(("parallel", "parallel", "arbitrary")),
    )(main, main, main, fcum, frow)


def _dsa_kernel(q_ref, k_ref, v_ref, iq_ref, ik_ref, iw_ref, o_ref,
                key_scr, bias_scr, kidx_scr, s_scr, *, tq, topk):
    i = pl.program_id(1)
    nch = i + 1

    @pl.when(i == 0)
    def _():
        kidx_scr[...] = ik_ref[...].astype(BF16)

    lane = lax.broadcasted_iota(jnp.int32, (tq, LANES), 1)
    first = lane < IDX_DIM
    row = lax.broadcasted_iota(jnp.int32, (tq, tq), 0)
    col = lax.broadcasted_iota(jnp.int32, (tq, tq), 1)
    iq = iq_ref[...]
    w = iw_ref[...] * ((N_IDX_HEADS * IDX_DIM) ** -0.5)
    qi, wh = [], []
    for h in range(N_IDX_HEADS):
        grp = iq[:, (h // 2) * LANES:(h // 2 + 1) * LANES]
        zero = jnp.zeros_like(grp)
        qi.append(jnp.where(first, grp, zero) if h % 2 == 0 else jnp.where(first, zero, grp))
        wh.append(w[:, N_FOX_HEADS + h:N_FOX_HEADS + h + 1])

    def score_chunk(c, diag):
        start = pl.multiple_of(c * tq, tq)
        kc = kidx_scr[pl.ds(start, tq), :]
        sc = jnp.zeros((tq, tq), F32)
        for h in range(N_IDX_HEADS):
            lg = lax.dot_general(qi[h], kc, (((1,), (1,)), ((), ())),
                                 preferred_element_type=F32)
            sc = sc + wh[h] * jnp.maximum(lg, 0.0)
        if diag:
            sc = jnp.where(col <= row, sc, -jnp.inf)
        sc = jnp.where(sc == 0.0, 0.0, sc)
        bits = pltpu.bitcast(sc, jnp.int32)
        key_scr[c] = bits ^ ((bits >> 31) & 0x7FFFFFFF)

    def plain_body(c, carry):
        score_chunk(c, False)
        return carry

    lax.fori_loop(0, i, plain_body, 0)
    score_chunk(i, True)

    kf = float(topk)
    ones_col = jnp.ones((tq, LANES), BF16)

    def wide(x):
        return jnp.concatenate([x] * (tq // LANES), axis=1)

    ones_sq = jnp.ones((LANES, LANES), BF16)

    def count(pred):
        def body(c, part):
            ones = jnp.where(pred(key_scr[c]), 1.0, 0.0).astype(BF16)
            for g in range(tq // LANES):
                part = part + ones[:, g * LANES:(g + 1) * LANES]
            return part
        part = lax.fori_loop(0, nch, body, jnp.zeros((tq, LANES), BF16))
        return jnp.dot(part, ones_sq, preferred_element_type=F32)

    def refine(cand, thr):
        cand_w = wide(cand)
        return jnp.where(count(lambda k: k >= cand_w) >= kf, cand, thr)

    thr = jnp.full((tq, LANES), INT_MIN, jnp.int32)
    thr = refine(jnp.zeros((tq, LANES), jnp.int32), thr)
    thr = lax.fori_loop(
        0, 31, lambda it, t: refine(t | jnp.left_shift(jnp.int32(1), 30 - it), t), thr)

    thr_w = wide(thr)
    need = kf - count(lambda k: k > thr_w)
    need_w = wide(jnp.where(thr == KEY_NEG_INF, 0.0, need))
    upper = jnp.where(row < col, 1.0, 0.0).astype(BF16)

    def sel_body(c, off):
        k = key_scr[c]
        eq = k == thr_w
        eqb = jnp.where(eq, 1.0, 0.0).astype(BF16)
        before = jnp.dot(eqb, upper, preferred_element_type=F32) + wide(off)
        tie = jnp.where(before < need_w, 0.0, NEG)
        bias_scr[c] = jnp.where(k > thr_w, 0.0, jnp.where(eq, tie, NEG))
        return off + jnp.dot(eqb, ones_col, preferred_element_type=F32)

    lax.fori_loop(0, nch, sel_body, jnp.zeros((tq, LANES), F32))

    for g in range(DSA_W // LANES):
        cols = slice(g * LANES, (g + 1) * LANES)

        def load_k(c, cols=cols):
            return k_ref[pl.ds(pl.multiple_of(c * tq, tq), tq), cols]

        def load_v(c, cols=cols):
            return v_ref[pl.ds(pl.multiple_of(c * tq, tq), tq), cols]

        def bias_fn(a, c, s, diag):
            return s + bias_scr[c]

        out = _flash_pair(q_ref[:, cols], load_k, load_v, bias_fn, nch, None, s_scr, tq)
        o_ref[:, cols] = out.astype(o_ref.dtype)


def _dsa_attention(main, small, batch, seq):
    tq = min(ATT_BLOCK, seq)
    nq = seq // tq
    n = batch * seq
    topk = min(DSA_TOPK_MAX, seq // 4)
    iq_block = MAIN_COLS // 256 - 1
    return pl.pallas_call(
        functools.partial(_dsa_kernel, tq=tq, topk=topk),
        grid=(batch, nq),
        in_specs=[pl.BlockSpec((tq, DSA_W), lambda b, i: (b * nq + i, 3)),
                  pl.BlockSpec((seq, DSA_W), lambda b, i: (b, 4)),
                  pl.BlockSpec((seq, DSA_W), lambda b, i: (b, 5)),
                  pl.BlockSpec((tq, 256), lambda b, i: (b * nq + i, iq_block)),
                  pl.BlockSpec((seq, LANES), lambda b, i: (b, 0)),
                  pl.BlockSpec((tq, LANES), lambda b, i: (b * nq + i, 1))],
        out_specs=pl.BlockSpec((tq, DSA_W), lambda b, i: (b * nq + i, 0)),
        out_shape=jax.ShapeDtypeStruct((n, DSA_W), BF16),
        scratch_shapes=[pltpu.VMEM((nq, tq, tq), jnp.int32),
                        pltpu.VMEM((nq, tq, tq), F32),
                        pltpu.VMEM((seq, LANES), BF16),
                        pltpu.VMEM((2, nq, tq, tq), F32)],
        compiler_params=_cparams(("parallel", "arbitrary")),
    )(main, main, main, main, small, small)


def _matmul_ln_kernel(*refs, n_pairs):
    a_refs = refs[:n_pairs]
    w_refs = refs[n_pairs:2 * n_pairs]
    bias_ref, x_ref, g_ref, b_ref, o_ref, ob_ref = refs[2 * n_pairs:]
    acc = bias_ref[...] + ALPHA * x_ref[...]
    for a_ref, w_ref in zip(a_refs, w_refs):
        acc = acc + jnp.dot(a_ref[...], w_ref[...], preferred_element_type=F32)
    y = _layer_norm(acc, g_ref[...], b_ref[...])
    o_ref[...] = y
    ob_ref[...] = y.astype(BF16)


def _matmul_ln(a_list, w_list, bias, x, g, b, tm):
    n, d = x.shape
    n_pairs = len(a_list)
    row = lambda i: (i, 0)
    fixed = lambda i: (0, 0)
    in_specs = ([pl.BlockSpec((tm, a.shape[1]), row) for a in a_list]
                + [pl.BlockSpec(w.shape, fixed) for w in w_list]
                + [pl.BlockSpec((1, d), fixed), pl.BlockSpec((tm, d), row),
                   pl.BlockSpec((1, d), fixed), pl.BlockSpec((1, d), fixed)])
    return pl.pallas_call(
        functools.partial(_matmul_ln_kernel, n_pairs=n_pairs),
        grid=(n // tm,),
        in_specs=in_specs,
        out_specs=[pl.BlockSpec((tm, d), row), pl.BlockSpec((tm, d), row)],
        out_shape=[jax.ShapeDtypeStruct((n, d), F32), jax.ShapeDtypeStruct((n, d), BF16)],
        compiler_params=_cparams(("parallel",)),
    )(*a_list, *w_list, bias, x, g, b)


def _swiglu_kernel(xb_ref, wg_ref, wu_ref, wd_ref, x_ref, g_ref, b_ref, o_ref, ob_ref, acc_ref):
    f = pl.program_id(1)

    @pl.when(f == 0)
    def _():
        acc_ref[...] = ALPHA * x_ref[...]

    xb = xb_ref[...]
    hg = jnp.dot(xb, wg_ref[...], preferred_element_type=F32)
    hu = jnp.dot(xb, wu_ref[...], preferred_element_type=F32)
    h = hg * _sigmoid(hg) * hu
    acc_ref[...] += jnp.dot(h.astype(BF16), wd_ref[...], preferred_element_type=F32)

    @pl.when(f == pl.num_programs(1) - 1)
    def _():
        y = _layer_norm(acc_ref[...], g_ref[...], b_ref[...])
        o_ref[...] = y
        ob_ref[...] = y.astype(BF16)


def _swiglu_ln(xb, x, wg, wu, wd, g, b, tm, tf):
    n, d = x.shape
    ffn = wg.shape[1]
    row = lambda i, f: (i, 0)
    fixed = lambda i, f: (0, 0)
    return pl.pallas_call(
        _swiglu_kernel,
        grid=(n // tm, ffn // tf),
        in_specs=[pl.BlockSpec((tm, d), row),
                  pl.BlockSpec((d, tf), lambda i, f: (0, f)),
                  pl.BlockSpec((d, tf), lambda i, f: (0, f)),
                  pl.BlockSpec((tf, d), lambda i, f: (f, 0)),
                  pl.BlockSpec((tm, d), row),
                  pl.BlockSpec((1, d), fixed), pl.BlockSpec((1, d), fixed)],
        out_specs=[pl.BlockSpec((tm, d), row), pl.BlockSpec((tm, d), row)],
        out_shape=[jax.ShapeDtypeStruct((n, d), F32), jax.ShapeDtypeStruct((n, d), BF16)],
        scratch_shapes=[pltpu.VMEM((tm, d), F32)],
        compiler_params=_cparams(("parallel", "arbitrary")),
    )(xb, wg, wu, wd, x, g, b)


def _glu_kernel(x_ref, wa_ref, wg_ref, ba_ref, bg_ref, o_ref):
    xb = x_ref[...]
    a = jnp.dot(xb, wa_ref[...], preferred_element_type=F32) + ba_ref[...]
    gate = jnp.dot(xb, wg_ref[...], preferred_element_type=F32) + bg_ref[...]
    o_ref[...] = a * _sigmoid(gate)


def _glu_proj(xb, w, bias, tm, tn):
    n, d = xb.shape
    c = w.shape[1] // 2
    nc = c // tn
    return pl.pallas_call(
        _glu_kernel,
        grid=(n // tm, nc),
        in_specs=[pl.BlockSpec((tm, d), lambda i, j: (i, 0)),
                  pl.BlockSpec((d, tn), lambda i, j: (0, j)),
                  pl.BlockSpec((d, tn), lambda i, j: (0, nc + j)),
                  pl.BlockSpec((1, tn), lambda i, j: (0, j)),
                  pl.BlockSpec((1, tn), lambda i, j: (0, nc + j))],
        out_specs=pl.BlockSpec((tm, tn), lambda i, j: (i, j)),
        out_shape=jax.ShapeDtypeStruct((n, c), F32),
        compiler_params=_cparams(("parallel", "arbitrary")),
    )(xb, w, w, bias, bias)


def _dwconv_kernel(cur_ref, prev_ref, w_ref, bdw_ref, g_ref, b_ref, o_ref, buf_ref, y_ref,
                   *, ts, halo):
    i = pl.program_id(1)
    prev = prev_ref[...]
    buf_ref[0:halo, :] = jnp.where(i == 0, jnp.zeros_like(prev), prev)
    buf_ref[halo:halo + ts, :] = cur_ref[...]
    base = halo - (CONV_WIDTH - 1)
    for cb in range(cur_ref.shape[1] // LANES):
        cols = slice(cb * LANES, (cb + 1) * LANES)
        acc = jnp.zeros((ts, LANES), F32)
        for j in range(CONV_WIDTH):
            acc = acc + w_ref[j:j + 1, cols] * buf_ref[base + j:base + j + ts, cols]
        y_ref[:, cols] = acc
    y = _layer_norm(y_ref[...] + bdw_ref[...], g_ref[...], b_ref[...])
    o_ref[...] = (y * _sigmoid(y)).astype(o_ref.dtype)


def _dwconv_ln_silu(h, w_dw, b_dw, g, b, batch, seq, ts):
    n, c = h.shape
    halo = 32
    nt = seq // ts
    ratio = ts // halo
    fixed = lambda bi, i: (0, 0)
    cur = lambda bi, i: (bi * nt + i, 0)
    prev = lambda bi, i: (jnp.maximum((bi * nt + i) * ratio - 1, 0), 0)
    return pl.pallas_call(
        functools.partial(_dwconv_kernel, ts=ts, halo=halo),
        grid=(batch, nt),
        in_specs=[pl.BlockSpec((ts, c), cur), pl.BlockSpec((halo, c), prev),
                  pl.BlockSpec((CONV_WIDTH, c), fixed),
                  pl.BlockSpec((1, c), fixed), pl.BlockSpec((1, c), fixed),
                  pl.BlockSpec((1, c), fixed)],
        out_specs=pl.BlockSpec((ts, c), cur),
        out_shape=jax.ShapeDtypeStruct((n, c), BF16),
        scratch_shapes=[pltpu.VMEM((halo + ts, c), F32), pltpu.VMEM((ts, c), F32)],
        compiler_params=_cparams(("parallel", "arbitrary")),
    )(h, h, w_dw, b_dw, g, b)


MOE_TILE = 256
MOE_EXPERT_TILE = 1024
NOT_ROUTED = -1e9


def _router_kernel(x_ref, w_ref, comb_ref, rank_ref, cum_ref, count_scr):
    @pl.when(pl.program_id(0) == 0)
    def _():
        count_scr[...] = jnp.zeros_like(count_scr)

    logits = jnp.dot(x_ref[...], w_ref[...], precision=lax.Precision.HIGHEST,
                     preferred_element_type=F32)
    tm = logits.shape[0]
    lane = lax.broadcasted_iota(jnp.int32, logits.shape, 1)
    logits = jnp.where(lane < N_EXPERTS, logits, -jnp.inf)
    m1 = jnp.max(logits, axis=1, keepdims=True)
    i1 = jnp.min(jnp.where(logits == m1, lane, LANES), axis=1, keepdims=True)
    rest = jnp.where(lane == i1, -jnp.inf, logits)
    m2 = jnp.max(rest, axis=1, keepdims=True)
    i2 = jnp.min(jnp.where(rest == m2, lane, LANES), axis=1, keepdims=True)
    e2 = jnp.exp(m2 - m1)
    g1 = 1.0 / (1.0 + e2)
    g2 = e2 / (1.0 + e2)
    comb_ref[...] = jnp.where(lane == i1, g1, 0.0) + jnp.where(lane == i2, g2, 0.0)

    routed = jnp.where(lane == i1, 1.0, jnp.where(lane == i2, 1.0, 0.0))
    r = lax.broadcasted_iota(jnp.int32, (tm, tm), 0)
    c = lax.broadcasted_iota(jnp.int32, (tm, tm), 1)
    earlier = jnp.where(c < r, 1.0, 0.0).astype(BF16)
    before = jnp.dot(earlier, routed.astype(BF16), preferred_element_type=F32)
    start = count_scr[0:1, :]
    rank_ref[...] = jnp.where(routed > 0.0, before + start, NOT_ROUTED)
    after = start + before[tm - 1:tm, :] + routed[tm - 1:tm, :]
    count_scr[...] = jnp.broadcast_to(after, count_scr.shape)
    cum_ref[...] = jnp.broadcast_to(after, cum_ref.shape)


def _router(x, w_router_padded, tm):
    n, d = x.shape
    row = lambda i: (i, 0)
    return pl.pallas_call(
        _router_kernel,
        grid=(n // tm,),
        in_specs=[pl.BlockSpec((tm, d), row),
                  pl.BlockSpec((d, LANES), lambda i: (0, 0))],
        out_specs=[pl.BlockSpec((tm, LANES), row), pl.BlockSpec((tm, LANES), row),
                   pl.BlockSpec((8, LANES), row)],
        out_shape=[jax.ShapeDtypeStruct((n, LANES), F32), jax.ShapeDtypeStruct((n, LANES), F32),
                   jax.ShapeDtypeStruct((n // tm * 8, LANES), F32)],
        scratch_shapes=[pltpu.VMEM((8, LANES), F32)],
        compiler_params=_cparams(("arbitrary",)),
    )(x, w_router_padded)


def _dispatch_kernel(g_idx, t_idx, valid, e_of_g, slot_ref, x_ref, comb_ref, xg_ref, gate_ref,
                     acc_ref, gacc_ref):
    i = pl.program_id(0)
    g = g_idx[i]
    first = jnp.logical_or(i == 0, g_idx[jnp.maximum(i - 1, 0)] != g)
    last = jnp.logical_or(i == pl.num_programs(0) - 1,
                          g_idx[jnp.minimum(i + 1, pl.num_programs(0) - 1)] != g)

    @pl.when(first)
    def _():
        acc_ref[...] = jnp.zeros_like(acc_ref)
        gacc_ref[...] = jnp.zeros_like(gacc_ref)

    @pl.when(valid[i] > 0)
    def _():
        t = acc_ref.shape[0]
        e = e_of_g[g]
        sub = lax.broadcasted_iota(jnp.int32, (8, t), 0)
        slot_row = jnp.sum(jnp.where(sub == e, slot_ref[...], 0.0), axis=0, keepdims=True)
        want = (g * t + lax.broadcasted_iota(jnp.int32, (t, t), 0)).astype(F32)
        sel = jnp.where(slot_row == want, 1.0, 0.0).astype(BF16)
        acc_ref[...] += jnp.dot(sel, x_ref[...], preferred_element_type=F32)
        comb = comb_ref[...]
        hi = comb.astype(BF16)
        lo = (comb - hi.astype(F32)).astype(BF16)
        gacc_ref[...] += (jnp.dot(sel, hi, preferred_element_type=F32)
                          + jnp.dot(sel, lo, preferred_element_type=F32))

    @pl.when(last)
    def _():
        xg_ref[...] = acc_ref[...].astype(xg_ref.dtype)
        gate_ref[...] = gacc_ref[...]


def _dispatch(work, e_of_g, slot_rows, xb, comb, n_slots):
    g_idx, t_idx, valid = work
    n, d = xb.shape
    t = MOE_TILE
    tok = lambda i, g, ti, v, e: (ti[i], 0)
    slot = lambda i, g, ti, v, e: (g[i], 0)
    return pl.pallas_call(
        _dispatch_kernel,
        grid_spec=pltpu.PrefetchScalarGridSpec(
            num_scalar_prefetch=4,
            grid=(g_idx.shape[0],),
            in_specs=[pl.BlockSpec((8, t), lambda i, g, ti, v, e: (0, ti[i])),
                      pl.BlockSpec((t, d), tok), pl.BlockSpec((t, LANES), tok)],
            out_specs=[pl.BlockSpec((t, d), slot), pl.BlockSpec((t, LANES), slot)],
            scratch_shapes=[pltpu.VMEM((t, d), F32), pltpu.VMEM((t, LANES), F32)]),
        out_shape=[jax.ShapeDtypeStruct((n_slots, d), BF16),
                   jax.ShapeDtypeStruct((n_slots, LANES), F32)],
        compiler_params=_cparams(("arbitrary",)),
    )(g_idx, t_idx, valid, e_of_g, slot_rows, xb, comb)


def _expert_kernel(e_of_big, nvalid, xg_ref, gate_ref, wg_ref, wu_ref, wd_ref, y_ref, acc_ref):
    b = pl.program_id(0)
    f = pl.program_id(1)
    nv = nvalid[b]

    @pl.when(nv > 0)
    def _():
        @pl.when(f == 0)
        def _():
            acc_ref[...] = jnp.zeros_like(acc_ref)

        tm = acc_ref.shape[0]
        e = e_of_big[b]
        live = lax.broadcasted_iota(jnp.int32, (tm, 1), 0) < nv
        xg = xg_ref[...]
        xg = jnp.where(live, xg, jnp.zeros_like(xg))
        gates = gate_ref[...]
        lane = lax.broadcasted_iota(jnp.int32, gates.shape, 1)
        ge = jnp.sum(jnp.where(jnp.logical_and(lane == e, live), gates, 0.0),
                     axis=1, keepdims=True)
        hg = jnp.dot(xg, wg_ref[...], preferred_element_type=F32)
        hu = jnp.dot(xg, wu_ref[...], preferred_element_type=F32)
        h = hg * _sigmoid(hg) * hu * ge
        acc_ref[...] += jnp.dot(h.astype(BF16), wd_ref[...], preferred_element_type=F32)

        @pl.when(f == pl.num_programs(1) - 1)
        def _():
            y_ref[...] = acc_ref[...].astype(y_ref.dtype)


def _expert_ffn(e_of_big, nvalid, xg, gate, wg, wu, wd, tf):
    n_slots, d = xg.shape
    edim = wg.shape[2]
    tm = MOE_EXPERT_TILE
    row = lambda b, f, e, nv: (b, 0)
    return pl.pallas_call(
        _expert_kernel,
        grid_spec=pltpu.PrefetchScalarGridSpec(
            num_scalar_prefetch=2,
            grid=(n_slots // tm, edim // tf),
            in_specs=[pl.BlockSpec((tm, d), row), pl.BlockSpec((tm, LANES), row),
                      pl.BlockSpec((None, d, tf), lambda b, f, e, nv: (e[b], 0, f)),
                      pl.BlockSpec((None, d, tf), lambda b, f, e, nv: (e[b], 0, f)),
                      pl.BlockSpec((None, tf, d), lambda b, f, e, nv: (e[b], f, 0))],
            out_specs=pl.BlockSpec((tm, d), row),
            scratch_shapes=[pltpu.VMEM((tm, d), F32)]),
        out_shape=jax.ShapeDtypeStruct((n_slots, d), BF16),
        compiler_params=_cparams(("parallel", "arbitrary")),
    )(e_of_big, nvalid, xg, gate, wg, wu, wd)


def _combine_kernel(t_idx, g_idx, valid, e_of_g, slot_ref, y_ref, x_ref, g_ref, b_ref,
                    o_ref, ob_ref, acc_ref):
    i = pl.program_id(0)
    ti = t_idx[i]
    first = jnp.logical_or(i == 0, t_idx[jnp.maximum(i - 1, 0)] != ti)
    last = jnp.logical_or(i == pl.num_programs(0) - 1,
                          t_idx[jnp.minimum(i + 1, pl.num_programs(0) - 1)] != ti)

    @pl.when(first)
    def _():
        acc_ref[...] = ALPHA * x_ref[...]

    @pl.when(valid[i] > 0)
    def _():
        t = acc_ref.shape[0]
        g = g_idx[i]
        e = e_of_g[g]
        slots = slot_ref[...]
        lane = lax.broadcasted_iota(jnp.int32, slots.shape, 1)
        slot_col = jnp.sum(jnp.where(lane == e, slots, 0.0), axis=1, keepdims=True)
        want = (g * t + lax.broadcasted_iota(jnp.int32, (t, t), 1)).astype(F32)
        sel = jnp.where(slot_col == want, 1.0, 0.0).astype(BF16)
        acc_ref[...] += jnp.dot(sel, y_ref[...], preferred_element_type=F32)

    @pl.when(last)
    def _():
        y = _layer_norm(acc_ref[...], g_ref[...], b_ref[...])
        o_ref[...] = y
        ob_ref[...] = y.astype(BF16)


def _combine_ln(work, e_of_g, slot_cols, y, x, g, b):
    t_idx, g_idx, valid = work
    n, d = x.shape
    t = MOE_TILE
    tok = lambda i, ti, gi, v, e: (ti[i], 0)
    fixed = lambda i, ti, gi, v, e: (0, 0)
    return pl.pallas_call(
        _combine_kernel,
        grid_spec=pltpu.PrefetchScalarGridSpec(
            num_scalar_prefetch=4,
            grid=(t_idx.shape[0],),
            in_specs=[pl.BlockSpec((t, LANES), tok),
                      pl.BlockSpec((t, d), lambda i, ti, gi, v, e: (gi[i], 0)),
                      pl.BlockSpec((t, d), tok),
                      pl.BlockSpec((1, d), fixed), pl.BlockSpec((1, d), fixed)],
            out_specs=[pl.BlockSpec((t, d), tok), pl.BlockSpec((t, d), tok)],
            scratch_shapes=[pltpu.VMEM((t, d), F32)]),
        out_shape=[jax.ShapeDtypeStruct((n, d), F32), jax.ShapeDtypeStruct((n, d), BF16)],
        compiler_params=_cparams(("arbitrary",)),
    )(t_idx, g_idx, valid, e_of_g, slot_cols, y, x, g, b)


def _routing_tables(rank, cum, n):
    t, te = MOE_TILE, MOE_EXPERT_TILE
    nt = n // t
    i32 = jnp.int32
    after = cum.reshape(nt, 8, LANES)[:, 0, :N_EXPERTS].astype(i32)
    before = jnp.concatenate([jnp.zeros((1, N_EXPERTS), i32), after[:-1]], axis=0)
    totals = after[-1]
    padded = (totals + te - 1) // te * te
    off = jnp.cumsum(padded) - padded
    n_slots = 2 * n + N_EXPERTS * te
    n_big = n_slots // te
    n_g = n_slots // t

    big_start = jnp.arange(n_big, dtype=i32) * te
    e_of_big = jnp.minimum(jnp.sum(big_start[:, None] >= (off + padded)[None, :], axis=1),
                           N_EXPERTS - 1).astype(i32)
    nvalid = jnp.clip(totals[e_of_big] - (big_start - off[e_of_big]), 0, te).astype(i32)
    e_of_g = jnp.repeat(e_of_big, te // t)

    off_row = jnp.concatenate([off.astype(F32), jnp.zeros((LANES - N_EXPERTS,), F32)])[None, :]
    slot_cols = rank + off_row
    slot_rows = jnp.transpose(slot_cols[:, :N_EXPERTS])

    has = after > before
    g0 = (off[None, :] + before) // t
    g1 = (off[None, :] + jnp.maximum(after, 1) - 1) // t
    t_ids = jnp.broadcast_to(jnp.arange(nt, dtype=i32)[:, None], (nt, N_EXPERTS))
    cand_g = jnp.stack([g0, g0 + 1], axis=-1).reshape(-1).astype(i32)
    cand_t = jnp.stack([t_ids, t_ids], axis=-1).reshape(-1)
    cand_ok = jnp.stack([has, jnp.logical_and(has, g1 > g0)], axis=-1).reshape(-1)
    n_items = min(N_EXPERTS * nt + n_g, 2 * N_EXPERTS * nt)
    count = jnp.sum(cand_ok.astype(i32))
    big = jnp.int32(2 ** 30)

    def ordered(major, minor, minor_range):
        order = jnp.argsort(jnp.where(cand_ok, major * minor_range + minor, big))[:n_items]
        pos = jnp.minimum(jnp.arange(n_items, dtype=i32), count - 1)
        pick = order[pos]
        ok = (jnp.arange(n_items, dtype=i32) < count).astype(i32)
        return cand_g[pick], cand_t[pick], ok

    dg, dt, dok = ordered(cand_g, cand_t, nt)
    cg, ct, cok = ordered(cand_t, cand_g, n_g)
    return dict(n_slots=n_slots, e_of_big=e_of_big, nvalid=nvalid, e_of_g=e_of_g,
                slot_cols=slot_cols, slot_rows=slot_rows,
                dispatch=(dg, dt, dok), combine=(ct, cg, cok))


def _moe_ln(xb, x, w_router_padded, wg, wu, wd, g, b, tf):
    n, _ = x.shape
    comb, rank, cum = _router(x, w_router_padded, MOE_TILE)
    rt = _routing_tables(rank, cum, n)
    xg, gate = _dispatch(rt["dispatch"], rt["e_of_g"], rt["slot_rows"], xb, comb, rt["n_slots"])
    y = _expert_ffn(rt["e_of_big"], rt["nvalid"], xg, gate, wg, wu, wd, tf)
    return _combine_ln(rt["combine"], rt["e_of_g"], rt["slot_cols"], y, x, g, b)


def _rope_patterns(seq):
    half = ROPE_DIM // 2
    inv_freq = ROPE_THETA ** (-jnp.arange(0, ROPE_DIM, 2, dtype=F32) / ROPE_DIM)
    ang = jnp.arange(seq, dtype=F32)[:, None] * inv_freq[None, :]
    cos, sin = jnp.cos(ang), jnp.sin(ang)
    rest = HEAD_DIM - ROPE_DIM
    c = jnp.concatenate([cos, cos, jnp.ones((seq, rest), F32)], axis=1)
    s1 = jnp.concatenate([-sin, jnp.zeros((seq, half + rest), F32)], axis=1)
    s2 = jnp.concatenate([jnp.zeros((seq, half), F32), sin, jnp.zeros((seq, rest), F32)], axis=1)
    return c, s1, s2


def _row_tile(n, want):
    return want if n % want == 0 else n


def _attention_layer(x, xb, w_in, b_f, w_out, g, b, tabs_main, tabs_small, batch, seq):
    n = batch * seq
    o = [0, 512, 1024, 1536, 1544, 2056, 2568, 3080, 3336, 3400, 3404]
    part = lambda k: w_in[:, o[k]:o[k + 1]]
    scale = HEAD_DIM ** -0.5
    w_main = jnp.concatenate([part(0) * scale, part(1), part(2), part(4) * scale, part(5),
                              part(6), part(7)], axis=1).astype(BF16)
    pad = jnp.zeros((D_MODEL, LANES - N_FOX_HEADS - N_IDX_HEADS), F32)
    w_small = jnp.concatenate([part(8), part(8), part(3), part(9), pad], axis=1).astype(BF16)
    tm = _row_tile(seq, 512)
    main = _project(xb, w_main, tabs_main, ROPE_TILES, MAIN_TN, BF16, seq, tm)
    small = _project(xb, w_small, tabs_small, (0,), LANES, F32, seq, tm)
    bias_row = jnp.concatenate([b_f.astype(F32), jnp.zeros((LANES - N_FOX_HEADS,), F32)])[None, :]
    fcum = _forget_cumsum(small, bias_row, batch, seq)
    tq = min(ATT_BLOCK, seq)
    frow = jnp.transpose(fcum.reshape(batch, seq, LANES)[:, :, :N_FOX_HEADS], (0, 2, 1))
    frow = frow.reshape(batch, N_FOX_HEADS, seq // tq, tq)
    fox = _fox_attention(main, fcum, frow, batch, seq)
    dsa = _dsa_attention(main, small, batch, seq)
    wo = w_out.astype(BF16)
    zero_bias = jnp.zeros((1, D_MODEL), F32)
    return _matmul_ln([fox, dsa], [wo[:FOX_W], wo[FOX_W:]], zero_bias, x, g, b,
                      _row_tile(n, 512))


def kernel(x, norm_mix_g, norm_mix_b, norm_ffn_g, norm_ffn_b, attn_w_in, attn_b_f, attn_w_out,
           ffn_w_gate, ffn_w_up, ffn_w_down, conv_w_in, conv_b_in, conv_w_dw, conv_b_dw,
           conv_ln_g, conv_ln_b, conv_w_out, conv_b_out, moe_w_router, moe_w_gate, moe_w_up,
           moe_w_down):
    batch, seq, d = x.shape
    n = batch * seq
    pats = _rope_patterns(seq)
    tabs_main = tuple(jnp.tile(p, (1, MAIN_TN // HEAD_DIM)) for p in pats)
    tabs_small = tuple(jnp.tile(p, (1, LANES // HEAD_DIM)) for p in pats)
    r2 = lambda v: v.astype(F32).reshape(1, -1)

    xf = x.reshape(n, d)
    xb = xf.astype(BF16)
    for layer in range(DEPTH):
        j = layer // 2
        gm, bm = r2(norm_mix_g[layer]), r2(norm_mix_b[layer])
        gf, bf = r2(norm_ffn_g[layer]), r2(norm_ffn_b[layer])
        if layer % 2 == 0:
            xf, xb = _attention_layer(xf, xb, attn_w_in[j], attn_b_f[j], attn_w_out[j], gm, bm,
                                      tabs_main, tabs_small, batch, seq)
            xf, xb = _swiglu_ln(xb, xf, ffn_w_gate[j].astype(BF16), ffn_w_up[j].astype(BF16),
                                ffn_w_down[j].astype(BF16), gf, bf, _row_tile(n, 1024), 256)
        else:
            h = _glu_proj(xb, conv_w_in[j].astype(BF16), r2(conv_b_in[j]), _row_tile(n, 512), 256)
            hb = _dwconv_ln_silu(h, conv_w_dw[j].astype(F32), r2(conv_b_dw[j]),
                                 r2(conv_ln_g[j]), r2(conv_ln_b[j]), batch, seq,
                                 min(256, seq))
            xf, xb = _matmul_ln([hb], [conv_w_out[j].astype(BF16)], r2(conv_b_out[j]), xf, gm, bm,
                                _row_tile(n, 512))
            w_r = jnp.concatenate([moe_w_router[j].astype(F32),
                                   jnp.zeros((d, LANES - N_EXPERTS), F32)], axis=1)
            xf, xb = _moe_ln(xb, xf, w_r, moe_w_gate[j].astype(BF16), moe_w_up[j].astype(BF16),
                             moe_w_down[j].astype(BF16), gf, bf, 512)
    return xf.reshape(batch, seq, d)
```

```python
import functools

import jax
import jax.numpy as jnp
from jax import lax
from jax.experimental import pallas as pl
from jax.experimental.pallas import tpu as pltpu

F32 = jnp.float32
BF16 = jnp.bfloat16

D_MODEL = 1024
DEPTH = 4
HEAD_DIM = 64
FOX_W = 512
DSA_W = 512
N_FOX_HEADS = 8
N_IDX_HEADS = 4
IDX_DIM = 64
DSA_TOPK_MAX = 256
ROPE_DIM = 16
ROPE_THETA = 500000.0
CONV_WIDTH = 31
N_EXPERTS = 8
LN_EPS = 1e-5
ALPHA = (2.0 * DEPTH) ** 0.25

LANES = 128
NEG = -1e30
INT_MIN = -2 ** 31
KEY_NEG_INF = -2 ** 31 + 0x007FFFFF
VMEM_LIMIT = 56 * 1024 * 1024

ATT_BLOCK = 512
MAIN_COLS = 3328
MAIN_TN = 256
ROPE_TILES = (6, 7, 8, 9, 12)


def _cparams(sem):
    return pltpu.CompilerParams(dimension_semantics=sem, vmem_limit_bytes=VMEM_LIMIT)


def _layer_norm(y, g, b):
    mu = jnp.mean(y, axis=-1, keepdims=True)
    d = y - mu
    var = jnp.mean(d * d, axis=-1, keepdims=True)
    return d * lax.rsqrt(var + LN_EPS) * g + b


def _sigmoid(z):
    return 1.0 / (1.0 + jnp.exp(-z))


def _rope(acc, c, s1, s2):
    tn = acc.shape[1]
    return acc * c + pltpu.roll(acc, tn - 8, 1) * s1 + pltpu.roll(acc, 8, 1) * s2


def _proj_kernel(x_ref, w_ref, c_ref, s1_ref, s2_ref, main_ref, small_ref):
    x = x_ref[...]
    for j in range(MAIN_COLS // MAIN_TN):
        cols = slice(j * MAIN_TN, (j + 1) * MAIN_TN)
        acc = jnp.dot(x, w_ref[:, cols], preferred_element_type=F32)
        if j in ROPE_TILES:
            acc = _rope(acc, c_ref[...], s1_ref[...], s2_ref[...])
        main_ref[:, cols] = acc.astype(main_ref.dtype)
    acc = jnp.dot(x, w_ref[:, MAIN_COLS:MAIN_COLS + 2 * LANES], preferred_element_type=F32)
    small_ref[:, :LANES] = _rope(acc[:, :LANES], c_ref[:, :LANES], s1_ref[:, :LANES],
                                 s2_ref[:, :LANES])
    small_ref[:, LANES:] = acc[:, LANES:]


def _project(xb, w, tables, seq, tm):
    n, d = xb.shape
    pos_blocks = seq // tm
    tab_spec = pl.BlockSpec((tm, MAIN_TN), lambda i: (i % pos_blocks, 0))
    row = lambda i: (i, 0)
    return pl.pallas_call(
        _proj_kernel,
        grid=(n // tm,),
        in_specs=[pl.BlockSpec((tm, d), row),
                  pl.BlockSpec(w.shape, lambda i: (0, 0)),
                  tab_spec, tab_spec, tab_spec],
        out_specs=[pl.BlockSpec((tm, MAIN_COLS), row), pl.BlockSpec((tm, 2 * LANES), row)],
        out_shape=[jax.ShapeDtypeStruct((n, MAIN_COLS), BF16),
                   jax.ShapeDtypeStruct((n, 2 * LANES), F32)],
        compiler_params=_cparams(("parallel",)),
    )(xb, w, *tables)


def _forget_kernel(z_ref, b_ref, o_ref, *, chunk):
    s = z_ref.shape[0]
    z = z_ref[...] + b_ref[...]
    logf = jnp.minimum(z, 0.0) - jnp.log(1.0 + jnp.exp(-jnp.abs(z)))
    r = lax.broadcasted_iota(jnp.int32, (chunk, chunk), 0)
    c = lax.broadcasted_iota(jnp.int32, (chunk, chunk), 1)
    tri = jnp.where(c <= r, 1.0, 0.0).astype(F32)
    carry = jnp.zeros((1, LANES), F32)
    for k in range(s // chunk):
        blk = logf[k * chunk:(k + 1) * chunk, :]
        cs = jnp.dot(tri, blk, precision=lax.Precision.HIGHEST,
                     preferred_element_type=F32) + carry
        o_ref[k * chunk:(k + 1) * chunk, :] = cs
        carry = cs[chunk - 1:chunk, :]


def _forget_cumsum(small, bias_row, batch, seq):
    chunk = min(256, seq)
    return pl.pallas_call(
        functools.partial(_forget_kernel, chunk=chunk),
        grid=(batch,),
        in_specs=[pl.BlockSpec((seq, LANES), lambda b: (b, 1)),
                  pl.BlockSpec((1, LANES), lambda b: (0, 0))],
        out_specs=pl.BlockSpec((seq, LANES), lambda b: (b, 0)),
        out_shape=jax.ShapeDtypeStruct((batch * seq, LANES), F32),
        compiler_params=_cparams(("parallel",)),
    )(small, bias_row)


def _flash_pair(q2, load_k, load_v, bias_fn, n_plain, diag_chunk, s_scr, tq):
    lane = lax.broadcasted_iota(jnp.int32, (tq, LANES), 1)
    first = lane < HEAD_DIM
    qa = (jnp.where(first, q2, jnp.zeros_like(q2)), jnp.where(first, jnp.zeros_like(q2), q2))
    folds = s_scr.shape[3] // LANES

    def score(c, mx, diag):
        kc = load_k(c)
        new = []
        for a in range(2):
            s = lax.dot_general(qa[a], kc, (((1,), (1,)), ((), ())),
                                preferred_element_type=F32)
            s = bias_fn(a, c, s, diag)
            s_scr[a, c] = s
            m = mx[a]
            for f in range(folds):
                m = jnp.maximum(m, s[:, f * LANES:(f + 1) * LANES])
            new.append(m)
        return tuple(new)

    mx = tuple(jnp.full((tq, LANES), NEG, F32) for _ in range(2))
    mx = lax.fori_loop(0, n_plain, lambda c, m: score(c, m, False), mx)
    n_all = n_plain
    if diag_chunk is not None:
        mx = score(diag_chunk, mx, True)
        n_all = n_plain + 1
    row_max = [jnp.max(m, axis=1, keepdims=True) for m in mx]

    def weigh(c, carry):
        vc = load_v(c)
        new = []
        for a in range(2):
            l, acc = carry[a]
            p = jnp.exp(s_scr[a, c] - row_max[a])
            for f in range(folds):
                l = l + p[:, f * LANES:(f + 1) * LANES]
            acc = acc + jnp.dot(p.astype(BF16), vc, preferred_element_type=F32)
            new.append((l, acc))
        return tuple(new)

    init = tuple((jnp.zeros((tq, LANES), F32), jnp.zeros((tq, LANES), F32)) for _ in range(2))
    carry = lax.fori_loop(0, n_all, weigh, init)
    o0 = carry[0][1] / jnp.sum(carry[0][0], axis=1, keepdims=True)
    o1 = carry[1][1] / jnp.sum(carry[1][0], axis=1, keepdims=True)
    return jnp.where(first, o0, o1)


def _fox_kernel(q_ref, k_ref, v_ref, fc_ref, fr_ref, o_ref, s_scr, *, tq):
    hp = pl.program_id(1)
    i = pl.program_id(2)
    lane = lax.broadcasted_iota(jnp.int32, (tq, LANES), 1)
    fc = fc_ref[...]
    fq = [jnp.sum(jnp.where(lane == hp * 2 + a, fc, 0.0), axis=1, keepdims=True)
          for a in range(2)]
    row = lax.broadcasted_iota(jnp.int32, (tq, tq), 0)
    col = lax.broadcasted_iota(jnp.int32, (tq, tq), 1)

    def load_k(c):
        return k_ref[pl.ds(pl.multiple_of(c * tq, tq), tq), :]

    def load_v(c):
        return v_ref[pl.ds(pl.multiple_of(c * tq, tq), tq), :]

    def bias_fn(a, c, s, diag):
        fk = fr_ref[0, pl.ds(hp * 2 + a, 1), pl.ds(c, 1), :].reshape(1, tq)
        s = s + fq[a] - fk
        if diag:
            s = jnp.where(col <= row, s, NEG)
        return s

    out = _flash_pair(q_ref[...], load_k, load_v, bias_fn, i, i, s_scr, tq)
    o_ref[...] = out.astype(o_ref.dtype)


def _fox_attention(main, fcum, frow, batch, seq):
    tq = min(ATT_BLOCK, seq)
    nq = seq // tq
    n = batch * seq
    pairs = FOX_W // LANES
    return pl.pallas_call(
        functools.partial(_fox_kernel, tq=tq),
        grid=(batch, pairs, nq),
        in_specs=[pl.BlockSpec((tq, LANES), lambda b, h, i: (b * nq + i, h)),
                  pl.BlockSpec((seq, LANES), lambda b, h, i: (b, pairs + h)),
                  pl.BlockSpec((seq, LANES), lambda b, h, i: (b, 2 * pairs + h)),
                  pl.BlockSpec((tq, LANES), lambda b, h, i: (b * nq + i, 0)),
                  pl.BlockSpec((1, N_FOX_HEADS, nq, tq), lambda b, h, i: (b, 0, 0, 0))],
        out_specs=pl.BlockSpec((tq, LANES), lambda b, h, i: (b * nq + i, h)),
        out_shape=jax.ShapeDtypeStruct((n, FOX_W), BF16),
        scratch_shapes=[pltpu.VMEM((2, nq, tq, tq), F32)],
        compiler_params=_cparams(("parallel", "parallel", "arbitrary")),
    )(main, main, main, fcum, frow)


def _dsa_kernel(q_ref, k_ref, v_ref, iq_ref, ik_ref, iw_ref, o_ref,
                key_scr, bias_scr, kidx_scr, s_scr, *, tq, topk):
    i = pl.program_id(1)
    nch = i + 1

    @pl.when(i == 0)
    def _():
        kidx_scr[...] = ik_ref[...].astype(BF16)

    lane = lax.broadcasted_iota(jnp.int32, (tq, LANES), 1)
    first = lane < IDX_DIM
    row = lax.broadcasted_iota(jnp.int32, (tq, tq), 0)
    col = lax.broadcasted_iota(jnp.int32, (tq, tq), 1)
    iq = iq_ref[...]
    w = iw_ref[...] * ((N_IDX_HEADS * IDX_DIM) ** -0.5)
    qi, wh = [], []
    for h in range(N_IDX_HEADS):
        grp = iq[:, (h // 2) * LANES:(h // 2 + 1) * LANES]
        zero = jnp.zeros_like(grp)
        qi.append(jnp.where(first, grp, zero) if h % 2 == 0 else jnp.where(first, zero, grp))
        wh.append(w[:, N_FOX_HEADS + h:N_FOX_HEADS + h + 1])

    def score_chunk(c, diag):
        start = pl.multiple_of(c * tq, tq)
        kc = kidx_scr[pl.ds(start, tq), :]
        sc = jnp.zeros((tq, tq), F32)
        for h in range(N_IDX_HEADS):
            lg = lax.dot_general(qi[h], kc, (((1,), (1,)), ((), ())),
                                 preferred_element_type=F32)
            sc = sc + wh[h] * jnp.maximum(lg, 0.0)
        if diag:
            sc = jnp.where(col <= row, sc, -jnp.inf)
        sc = jnp.where(sc == 0.0, 0.0, sc)
        bits = pltpu.bitcast(sc, jnp.int32)
        key_scr[c] = bits ^ ((bits >> 31) & 0x7FFFFFFF)

    def plain_body(c, carry):
        score_chunk(c, False)
        return carry

    lax.fori_loop(0, i, plain_body, 0)
    score_chunk(i, True)

    kf = float(topk)
    ones_col = jnp.ones((tq, LANES), BF16)

    def wide(x):
        return jnp.concatenate([x] * (tq // LANES), axis=1)

    ones_sq = jnp.ones((LANES, LANES), BF16)

    def count(pred):
        def body(c, part):
            ones = jnp.where(pred(key_scr[c]), 1.0, 0.0).astype(BF16)
            for g in range(tq // LANES):
                part = part + ones[:, g * LANES:(g + 1) * LANES]
            return part
        part = lax.fori_loop(0, nch, body, jnp.zeros((tq, LANES), BF16))
        return jnp.dot(part, ones_sq, preferred_element_type=F32)

    def refine(cand, thr):
        cand_w = wide(cand)
        return jnp.where(count(lambda k: k >= cand_w) >= kf, cand, thr)

    thr = jnp.full((tq, LANES), INT_MIN, jnp.int32)
    thr = refine(jnp.zeros((tq, LANES), jnp.int32), thr)
    thr = lax.fori_loop(
        0, 31, lambda it, t: refine(t | jnp.left_shift(jnp.int32(1), 30 - it), t), thr)

    thr_w = wide(thr)
    need = kf - count(lambda k: k > thr_w)
    need_w = wide(jnp.where(thr == KEY_NEG_INF, 0.0, need))
    upper = jnp.where(row < col, 1.0, 0.0).astype(BF16)

    def sel_body(c, off):
        k = key_scr[c]
        eq = k == thr_w
        eqb = jnp.where(eq, 1.0, 0.0).astype(BF16)
        before = jnp.dot(eqb, upper, preferred_element_type=F32) + wide(off)
        tie = jnp.where(before < need_w, 0.0, NEG)
        bias_scr[c] = jnp.where(k > thr_w, 0.0, jnp.where(eq, tie, NEG))
        return off + jnp.dot(eqb, ones_col, preferred_element_type=F32)

    lax.fori_loop(0, nch, sel_body, jnp.zeros((tq, LANES), F32))

    for g in range(DSA_W // LANES):
        cols = slice(g * LANES, (g + 1) * LANES)

        def load_k(c, cols=cols):
            return k_ref[pl.ds(pl.multiple_of(c * tq, tq), tq), cols]

        def load_v(c, cols=cols):
            return v_ref[pl.ds(pl.multiple_of(c * tq, tq), tq), cols]

        def bias_fn(a, c, s, diag):
            return s + bias_scr[c]

        out = _flash_pair(q_ref[:, cols], load_k, load_v, bias_fn, nch, None, s_scr, tq)
        o_ref[:, cols] = out.astype(o_ref.dtype)


def _dsa_attention(main, small, batch, seq):
    tq = min(ATT_BLOCK, seq)
    nq = seq // tq
    n = batch * seq
    topk = min(DSA_TOPK_MAX, seq // 4)
    iq_block = MAIN_COLS // 256 - 1
    return pl.pallas_call(
        functools.partial(_dsa_kernel, tq=tq, topk=topk),
        grid=(batch, nq),
        in_specs=[pl.BlockSpec((tq, DSA_W), lambda b, i: (b * nq + i, 3)),
                  pl.BlockSpec((seq, DSA_W), lambda b, i: (b, 4)),
                  pl.BlockSpec((seq, DSA_W), lambda b, i: (b, 5)),
                  pl.BlockSpec((tq, 256), lambda b, i: (b * nq + i, iq_block)),
                  pl.BlockSpec((seq, LANES), lambda b, i: (b, 0)),
                  pl.BlockSpec((tq, LANES), lambda b, i: (b * nq + i, 1))],
        out_specs=pl.BlockSpec((tq, DSA_W), lambda b, i: (b * nq + i, 0)),
        out_shape=jax.ShapeDtypeStruct((n, DSA_W), BF16),
        scratch_shapes=[pltpu.VMEM((nq, tq, tq), jnp.int32),
                        pltpu.VMEM((nq, tq, tq), F32),
                        pltpu.VMEM((seq, LANES), BF16),
                        pltpu.VMEM((2, nq, tq, tq), F32)],
        compiler_params=_cparams(("parallel", "arbitrary")),
    )(main, main, main, main, small, small)


def _matmul_ln_kernel(*refs, n_pairs):
    a_refs = refs[:n_pairs]
    w_refs = refs[n_pairs:2 * n_pairs]
    bias_ref, x_ref, g_ref, b_ref, o_ref, ob_ref = refs[2 * n_pairs:]
    acc = bias_ref[...] + ALPHA * x_ref[...]
    for a_ref, w_ref in zip(a_refs, w_refs):
        acc = acc + jnp.dot(a_ref[...], w_ref[...], preferred_element_type=F32)
    y = _layer_norm(acc, g_ref[...], b_ref[...])
    o_ref[...] = y
    ob_ref[...] = y.astype(BF16)


def _matmul_ln(a_list, w_list, bias, x, g, b, tm):
    n, d = x.shape
    n_pairs = len(a_list)
    row = lambda i: (i, 0)
    fixed = lambda i: (0, 0)
    in_specs = ([pl.BlockSpec((tm, a.shape[1]), row) for a in a_list]
                + [pl.BlockSpec(w.shape, fixed) for w in w_list]
                + [pl.BlockSpec((1, d), fixed), pl.BlockSpec((tm, d), row),
                   pl.BlockSpec((1, d), fixed), pl.BlockSpec((1, d), fixed)])
    return pl.pallas_call(
        functools.partial(_matmul_ln_kernel, n_pairs=n_pairs),
        grid=(n // tm,),
        in_specs=in_specs,
        out_specs=[pl.BlockSpec((tm, d), row), pl.BlockSpec((tm, d), row)],
        out_shape=[jax.ShapeDtypeStruct((n, d), F32), jax.ShapeDtypeStruct((n, d), BF16)],
        compiler_params=_cparams(("parallel",)),
    )(*a_list, *w_list, bias, x, g, b)


def _swiglu_kernel(xb_ref, wg_ref, wu_ref, wd_ref, x_ref, g_ref, b_ref, o_ref, ob_ref, acc_ref):
    f = pl.program_id(1)

    @pl.when(f == 0)
    def _():
        acc_ref[...] = ALPHA * x_ref[...]

    xb = xb_ref[...]
    hg = jnp.dot(xb, wg_ref[...], preferred_element_type=F32)
    hu = jnp.dot(xb, wu_ref[...], preferred_element_type=F32)
    h = hg * _sigmoid(hg) * hu
    acc_ref[...] += jnp.dot(h.astype(BF16), wd_ref[...], preferred_element_type=F32)

    @pl.when(f == pl.num_programs(1) - 1)
    def _():
        y = _layer_norm(acc_ref[...], g_ref[...], b_ref[...])
        o_ref[...] = y
        ob_ref[...] = y.astype(BF16)


def _swiglu_ln(xb, x, wg, wu, wd, g, b, tm, tf):
    n, d = x.shape
    ffn = wg.shape[1]
    row = lambda i, f: (i, 0)
    fixed = lambda i, f: (0, 0)
    return pl.pallas_call(
        _swiglu_kernel,
        grid=(n // tm, ffn // tf),
        in_specs=[pl.BlockSpec((tm, d), row),
                  pl.BlockSpec((d, tf), lambda i, f: (0, f)),
                  pl.BlockSpec((d, tf), lambda i, f: (0, f)),
                  pl.BlockSpec((tf, d), lambda i, f: (f, 0)),
                  pl.BlockSpec((tm, d), row),
                  pl.BlockSpec((1, d), fixed), pl.BlockSpec((1, d), fixed)],
        out_specs=[pl.BlockSpec((tm, d), row), pl.BlockSpec((tm, d), row)],
        out_shape=[jax.ShapeDtypeStruct((n, d), F32), jax.ShapeDtypeStruct((n, d), BF16)],
        scratch_shapes=[pltpu.VMEM((tm, d), F32)],
        compiler_params=_cparams(("parallel", "arbitrary")),
    )(xb, wg, wu, wd, x, g, b)


def _glu_kernel(x_ref, wa_ref, wg_ref, ba_ref, bg_ref, o_ref):
    xb = x_ref[...]
    a = jnp.dot(xb, wa_ref[...], preferred_element_type=F32) + ba_ref[...]
    gate = jnp.dot(xb, wg_ref[...], preferred_element_type=F32) + bg_ref[...]
    o_ref[...] = a * _sigmoid(gate)


def _glu_proj(xb, w, bias, tm, tn):
    n, d = xb.shape
    c = w.shape[1] // 2
    nc = c // tn
    return pl.pallas_call(
        _glu_kernel,
        grid=(n // tm, nc),
        in_specs=[pl.BlockSpec((tm, d), lambda i, j: (i, 0)),
                  pl.BlockSpec((d, tn), lambda i, j: (0, j)),
                  pl.BlockSpec((d, tn), lambda i, j: (0, nc + j)),
                  pl.BlockSpec((1, tn), lambda i, j: (0, j)),
                  pl.BlockSpec((1, tn), lambda i, j: (0, nc + j))],
        out_specs=pl.BlockSpec((tm, tn), lambda i, j: (i, j)),
        out_shape=jax.ShapeDtypeStruct((n, c), F32),
        compiler_params=_cparams(("parallel", "arbitrary")),
    )(xb, w, w, bias, bias)


def _dwconv_kernel(cur_ref, prev_ref, w_ref, bdw_ref, g_ref, b_ref, o_ref, buf_ref, sh_ref, y_ref,
                   *, ts, halo):
    i = pl.program_id(1)
    prev = prev_ref[...]
    buf_ref[0:halo, :] = jnp.where(i == 0, jnp.zeros_like(prev), prev)
    buf_ref[halo:halo + ts, :] = cur_ref[...]
    base = halo - (CONV_WIDTH - 1)
    span = sh_ref.shape[1]
    for cb in range(cur_ref.shape[1] // LANES):
        cols = slice(cb * LANES, (cb + 1) * LANES)
        for r in range(1, 8):
            sh_ref[r - 1] = buf_ref[r:r + span, cols]
        acc = jnp.zeros((ts, LANES), F32)
        for j in range(CONV_WIDTH):
            a, r = divmod(base + j, 8)
            if r == 0:
                win = buf_ref[8 * a:8 * a + ts, cols]
            else:
                win = sh_ref[r - 1, 8 * a:8 * a + ts, :]
            acc = acc + w_ref[j:j + 1, cols] * win
        y_ref[:, cols] = acc
    y = _layer_norm(y_ref[...] + bdw_ref[...], g_ref[...], b_ref[...])
    o_ref[...] = (y * _sigmoid(y)).astype(o_ref.dtype)


def _dwconv_ln_silu(h, w_dw, b_dw, g, b, batch, seq, ts):
    n, c = h.shape
    halo = 32
    nt = seq // ts
    ratio = ts // halo
    fixed = lambda bi, i: (0, 0)
    cur = lambda bi, i: (bi * nt + i, 0)
    prev = lambda bi, i: (jnp.maximum((bi * nt + i) * ratio - 1, 0), 0)
    return pl.pallas_call(
        functools.partial(_dwconv_kernel, ts=ts, halo=halo),
        grid=(batch, nt),
        in_specs=[pl.BlockSpec((ts, c), cur), pl.BlockSpec((halo, c), prev),
                  pl.BlockSpec((CONV_WIDTH, c), fixed),
                  pl.BlockSpec((1, c), fixed), pl.BlockSpec((1, c), fixed),
                  pl.BlockSpec((1, c), fixed)],
        out_specs=pl.BlockSpec((ts, c), cur),
        out_shape=jax.ShapeDtypeStruct((n, c), BF16),
        scratch_shapes=[pltpu.VMEM((halo + ts, c), F32),
                        pltpu.VMEM((7, ts + halo - 8, LANES), F32),
                        pltpu.VMEM((ts, c), F32)],
        compiler_params=_cparams(("parallel", "arbitrary")),
    )(h, h, w_dw, b_dw, g, b)


MOE_TILE = 512
MOE_EXPERT_TILE = 1024
NOT_ROUTED = -1e9


def _router_kernel(x_ref, w_ref, comb_ref, rank_ref, cum_ref, count_scr):
    @pl.when(pl.program_id(0) == 0)
    def _():
        count_scr[...] = jnp.zeros_like(count_scr)

    logits = jnp.dot(x_ref[...], w_ref[...], precision=lax.Precision.HIGHEST,
                     preferred_element_type=F32)
    tm = logits.shape[0]
    lane = lax.broadcasted_iota(jnp.int32, logits.shape, 1)
    logits = jnp.where(lane < N_EXPERTS, logits, -jnp.inf)
    m1 = jnp.max(logits, axis=1, keepdims=True)
    i1 = jnp.min(jnp.where(logits == m1, lane, LANES), axis=1, keepdims=True)
    rest = jnp.where(lane == i1, -jnp.inf, logits)
    m2 = jnp.max(rest, axis=1, keepdims=True)
    i2 = jnp.min(jnp.where(rest == m2, lane, LANES), axis=1, keepdims=True)
    e2 = jnp.exp(m2 - m1)
    g1 = 1.0 / (1.0 + e2)
    g2 = e2 / (1.0 + e2)
    comb_ref[...] = jnp.where(lane == i1, g1, 0.0) + jnp.where(lane == i2, g2, 0.0)

    routed = jnp.where(lane == i1, 1.0, jnp.where(lane == i2, 1.0, 0.0))
    r = lax.broadcasted_iota(jnp.int32, (tm, tm), 0)
    c = lax.broadcasted_iota(jnp.int32, (tm, tm), 1)
    earlier = jnp.where(c < r, 1.0, 0.0).astype(BF16)
    before = jnp.dot(earlier, routed.astype(BF16), preferred_element_type=F32)
    start = count_scr[0:1, :]
    rank_ref[...] = jnp.where(routed > 0.0, before + start, NOT_ROUTED)
    after = start + before[tm - 1:tm, :] + routed[tm - 1:tm, :]
    count_scr[...] = jnp.broadcast_to(after, count_scr.shape)
    cum_ref[...] = jnp.broadcast_to(after, cum_ref.shape)


def _router(x, w_router_padded, tm):
    n, d = x.shape
    row = lambda i: (i, 0)
    return pl.pallas_call(
        _router_kernel,
        grid=(n // tm,),
        in_specs=[pl.BlockSpec((tm, d), row),
                  pl.BlockSpec((d, LANES), lambda i: (0, 0))],
        out_specs=[pl.BlockSpec((tm, LANES), row), pl.BlockSpec((tm, LANES), row),
                   pl.BlockSpec((8, LANES), row)],
        out_shape=[jax.ShapeDtypeStruct((n, LANES), F32), jax.ShapeDtypeStruct((n, LANES), F32),
                   jax.ShapeDtypeStruct((n // tm * 8, LANES), F32)],
        scratch_shapes=[pltpu.VMEM((8, LANES), F32)],
        compiler_params=_cparams(("arbitrary",)),
    )(x, w_router_padded)


def _dispatch_kernel(g_idx, t_idx, valid, e_of_g, slot_ref, x_ref, comb_ref, xg_ref, gate_ref,
                     acc_ref, gacc_ref):
    i = pl.program_id(0)
    g = g_idx[i]
    first = jnp.logical_or(i == 0, g_idx[jnp.maximum(i - 1, 0)] != g)
    last = jnp.logical_or(i == pl.num_programs(0) - 1,
                          g_idx[jnp.minimum(i + 1, pl.num_programs(0) - 1)] != g)

    @pl.when(first)
    def _():
        acc_ref[...] = jnp.zeros_like(acc_ref)
        gacc_ref[...] = jnp.zeros_like(gacc_ref)

    @pl.when(valid[i] > 0)
    def _():
        t = acc_ref.shape[0]
        e = e_of_g[g]
        sub = lax.broadcasted_iota(jnp.int32, (8, t), 0)
        slot_row = jnp.sum(jnp.where(sub == e, slot_ref[...], 0.0), axis=0, keepdims=True)
        want = (g * t + lax.broadcasted_iota(jnp.int32, (t, t), 0)).astype(F32)
        sel = jnp.where(slot_row == want, 1.0, 0.0).astype(BF16)
        acc_ref[...] += jnp.dot(sel, x_ref[...], preferred_element_type=F32)
        comb = comb_ref[...]
        hi = comb.astype(BF16)
        lo = (comb - hi.astype(F32)).astype(BF16)
        gacc_ref[...] += (jnp.dot(sel, hi, preferred_element_type=F32)
                          + jnp.dot(sel, lo, preferred_element_type=F32))

    @pl.when(last)
    def _():
        xg_ref[...] = acc_ref[...].astype(xg_ref.dtype)
        gate_ref[...] = gacc_ref[...]


def _dispatch(work, e_of_g, slot_rows, xb, comb, n_slots):
    g_idx, t_idx, valid = work
    n, d = xb.shape
    t = MOE_TILE
    tok = lambda i, g, ti, v, e: (ti[i], 0)
    slot = lambda i, g, ti, v, e: (g[i], 0)
    return pl.pallas_call(
        _dispatch_kernel,
        grid_spec=pltpu.PrefetchScalarGridSpec(
            num_scalar_prefetch=4,
            grid=(g_idx.shape[0],),
            in_specs=[pl.BlockSpec((8, t), lambda i, g, ti, v, e: (0, ti[i])),
                      pl.BlockSpec((t, d), tok), pl.BlockSpec((t, LANES), tok)],
            out_specs=[pl.BlockSpec((t, d), slot), pl.BlockSpec((t, LANES), slot)],
            scratch_shapes=[pltpu.VMEM((t, d), F32), pltpu.VMEM((t, LANES), F32)]),
        out_shape=[jax.ShapeDtypeStruct((n_slots, d), BF16),
                   jax.ShapeDtypeStruct((n_slots, LANES), F32)],
        compiler_params=_cparams(("arbitrary",)),
    )(g_idx, t_idx, valid, e_of_g, slot_rows, xb, comb)


def _expert_kernel(e_of_big, nvalid, xg_ref, gate_ref, wg_ref, wu_ref, wd_ref, y_ref, acc_ref):
    b = pl.program_id(0)
    f = pl.program_id(1)
    nv = nvalid[b]

    @pl.when(nv > 0)
    def _():
        @pl.when(f == 0)
        def _():
            acc_ref[...] = jnp.zeros_like(acc_ref)

        tm = acc_ref.shape[0]
        e = e_of_big[b]
        live = lax.broadcasted_iota(jnp.int32, (tm, 1), 0) < nv
        xg = xg_ref[...]
        xg = jnp.where(live, xg, jnp.zeros_like(xg))
        gates = gate_ref[...]
        lane = lax.broadcasted_iota(jnp.int32, gates.shape, 1)
        ge = jnp.sum(jnp.where(jnp.logical_and(lane == e, live), gates, 0.0),
                     axis=1, keepdims=True)
        hg = jnp.dot(xg, wg_ref[...], preferred_element_type=F32)
        hu = jnp.dot(xg, wu_ref[...], preferred_element_type=F32)
        h = hg * _sigmoid(hg) * hu * ge
        acc_ref[...] += jnp.dot(h.astype(BF16), wd_ref[...], preferred_element_type=F32)

        @pl.when(f == pl.num_programs(1) - 1)
        def _():
            y_ref[...] = acc_ref[...].astype(y_ref.dtype)


def _expert_ffn(e_of_big, nvalid, xg, gate, wg, wu, wd, tf):
    n_slots, d = xg.shape
    edim = wg.shape[2]
    tm = MOE_EXPERT_TILE
    row = lambda b, f, e, nv: (b, 0)
    return pl.pallas_call(
        _expert_kernel,
        grid_spec=pltpu.PrefetchScalarGridSpec(
            num_scalar_prefetch=2,
            grid=(n_slots // tm, edim // tf),
            in_specs=[pl.BlockSpec((tm, d), row), pl.BlockSpec((tm, LANES), row),
                      pl.BlockSpec((None, d, tf), lambda b, f, e, nv: (e[b], 0, f)),
                      pl.BlockSpec((None, d, tf), lambda b, f, e, nv: (e[b], 0, f)),
                      pl.BlockSpec((None, tf, d), lambda b, f, e, nv: (e[b], f, 0))],
            out_specs=pl.BlockSpec((tm, d), row),
            scratch_shapes=[pltpu.VMEM((tm, d), F32)]),
        out_shape=jax.ShapeDtypeStruct((n_slots, d), BF16),
        compiler_params=_cparams(("parallel", "arbitrary")),
    )(e_of_big, nvalid, xg, gate, wg, wu, wd)


def _combine_kernel(t_idx, g_idx, valid, e_of_g, slot_ref, y_ref, x_ref, g_ref, b_ref,
                    o_ref, ob_ref, acc_ref):
    i = pl.program_id(0)
    ti = t_idx[i]
    first = jnp.logical_or(i == 0, t_idx[jnp.maximum(i - 1, 0)] != ti)
    last = jnp.logical_or(i == pl.num_programs(0) - 1,
                          t_idx[jnp.minimum(i + 1, pl.num_programs(0) - 1)] != ti)

    @pl.when(first)
    def _():
        acc_ref[...] = ALPHA * x_ref[...]

    @pl.when(valid[i] > 0)
    def _():
        t = acc_ref.shape[0]
        g = g_idx[i]
        e = e_of_g[g]
        slots = slot_ref[...]
        lane = lax.broadcasted_iota(jnp.int32, slots.shape, 1)
        slot_col = jnp.sum(jnp.where(lane == e, slots, 0.0), axis=1, keepdims=True)
        want = (g * t + lax.broadcasted_iota(jnp.int32, (t, t), 1)).astype(F32)
        sel = jnp.where(slot_col == want, 1.0, 0.0).astype(BF16)
        acc_ref[...] += jnp.dot(sel, y_ref[...], preferred_element_type=F32)

    @pl.when(last)
    def _():
        y = _layer_norm(acc_ref[...], g_ref[...], b_ref[...])
        o_ref[...] = y
        ob_ref[...] = y.astype(BF16)


def _combine_ln(work, e_of_g, slot_cols, y, x, g, b):
    t_idx, g_idx, valid = work
    n, d = x.shape
    t = MOE_TILE
    tok = lambda i, ti, gi, v, e: (ti[i], 0)
    fixed = lambda i, ti, gi, v, e: (0, 0)
    return pl.pallas_call(
        _combine_kernel,
        grid_spec=pltpu.PrefetchScalarGridSpec(
            num_scalar_prefetch=4,
            grid=(t_idx.shape[0],),
            in_specs=[pl.BlockSpec((t, LANES), tok),
                      pl.BlockSpec((t, d), lambda i, ti, gi, v, e: (gi[i], 0)),
                      pl.BlockSpec((t, d), tok),
                      pl.BlockSpec((1, d), fixed), pl.BlockSpec((1, d), fixed)],
            out_specs=[pl.BlockSpec((t, d), tok), pl.BlockSpec((t, d), tok)],
            scratch_shapes=[pltpu.VMEM((t, d), F32)]),
        out_shape=[jax.ShapeDtypeStruct((n, d), F32), jax.ShapeDtypeStruct((n, d), BF16)],
        compiler_params=_cparams(("arbitrary",)),
    )(t_idx, g_idx, valid, e_of_g, slot_cols, y, x, g, b)


def _routing_tables(rank, cum, n):
    t, te = MOE_TILE, MOE_EXPERT_TILE
    nt = n // t
    i32 = jnp.int32
    after = cum.reshape(nt, 8, LANES)[:, 0, :N_EXPERTS].astype(i32)
    before = jnp.concatenate([jnp.zeros((1, N_EXPERTS), i32), after[:-1]], axis=0)
    totals = after[-1]
    padded = (totals + te - 1) // te * te
    off = jnp.cumsum(padded) - padded
    n_slots = 2 * n + N_EXPERTS * te
    n_big = n_slots // te
    n_g = n_slots // t

    big_start = jnp.arange(n_big, dtype=i32) * te
    e_of_big = jnp.minimum(jnp.sum(big_start[:, None] >= (off + padded)[None, :], axis=1),
                           N_EXPERTS - 1).astype(i32)
    nvalid = jnp.clip(totals[e_of_big] - (big_start - off[e_of_big]), 0, te).astype(i32)
    e_of_g = jnp.repeat(e_of_big, te // t)

    off_row = jnp.concatenate([off.astype(F32), jnp.zeros((LANES - N_EXPERTS,), F32)])[None, :]
    slot_cols = rank + off_row
    slot_rows = jnp.transpose(slot_cols[:, :N_EXPERTS])

    has = after > before
    g0 = (off[None, :] + before) // t
    g1 = (off[None, :] + jnp.maximum(after, 1) - 1) // t
    t_ids = jnp.broadcast_to(jnp.arange(nt, dtype=i32)[:, None], (nt, N_EXPERTS))
    cand_g = jnp.stack([g0, g0 + 1], axis=-1).reshape(-1).astype(i32)
    cand_t = jnp.stack([t_ids, t_ids], axis=-1).reshape(-1)
    cand_ok = jnp.stack([has, jnp.logical_and(has, g1 > g0)], axis=-1).reshape(-1)
    n_items = min(N_EXPERTS * nt + n_g, 2 * N_EXPERTS * nt)
    count = jnp.sum(cand_ok.astype(i32))
    big = jnp.int32(2 ** 30)

    def ordered(major, minor, minor_range):
        order = jnp.argsort(jnp.where(cand_ok, major * minor_range + minor, big))[:n_items]
        pos = jnp.minimum(jnp.arange(n_items, dtype=i32), count - 1)
        pick = order[pos]
        ok = (jnp.arange(n_items, dtype=i32) < count).astype(i32)
        return cand_g[pick], cand_t[pick], ok

    dg, dt, dok = ordered(cand_g, cand_t, nt)
    cg, ct, cok = ordered(cand_t, cand_g, n_g)
    return dict(n_slots=n_slots, e_of_big=e_of_big, nvalid=nvalid, e_of_g=e_of_g,
                slot_cols=slot_cols, slot_rows=slot_rows,
                dispatch=(dg, dt, dok), combine=(ct, cg, cok))


def _moe_ln(xb, x, w_router_padded, wg, wu, wd, g, b, tf):
    n, _ = x.shape
    comb, rank, cum = _router(x, w_router_padded, MOE_TILE)
    rt = _routing_tables(rank, cum, n)
    xg, gate = _dispatch(rt["dispatch"], rt["e_of_g"], rt["slot_rows"], xb, comb, rt["n_slots"])
    y = _expert_ffn(rt["e_of_big"], rt["nvalid"], xg, gate, wg, wu, wd, tf)
    return _combine_ln(rt["combine"], rt["e_of_g"], rt["slot_cols"], y, x, g, b)


def _rope_patterns(seq):
    half = ROPE_DIM // 2
    inv_freq = ROPE_THETA ** (-jnp.arange(0, ROPE_DIM, 2, dtype=F32) / ROPE_DIM)
    ang = jnp.arange(seq, dtype=F32)[:, None] * inv_freq[None, :]
    cos, sin = jnp.cos(ang), jnp.sin(ang)
    rest = HEAD_DIM - ROPE_DIM
    c = jnp.concatenate([cos, cos, jnp.ones((seq, rest), F32)], axis=1)
    s1 = jnp.concatenate([-sin, jnp.zeros((seq, half + rest), F32)], axis=1)
    s2 = jnp.concatenate([jnp.zeros((seq, half), F32), sin, jnp.zeros((seq, rest), F32)], axis=1)
    return c, s1, s2


def _row_tile(n, want):
    return want if n % want == 0 else n


def _attention_layer(x, xb, w_in, b_f, w_out, g, b, tabs, batch, seq):
    n = batch * seq
    o = [0, 512, 1024, 1536, 1544, 2056, 2568, 3080, 3336, 3400, 3404]
    part = lambda k: w_in[:, o[k]:o[k + 1]]
    scale = HEAD_DIM ** -0.5
    pad = jnp.zeros((D_MODEL, LANES - N_FOX_HEADS - N_IDX_HEADS), F32)
    w_all = jnp.concatenate([part(0) * scale, part(1), part(2), part(4) * scale, part(5),
                             part(6), part(7), part(8), part(8), part(3), part(9), pad],
                            axis=1).astype(BF16)
    main, small = _project(xb, w_all, tabs, seq, _row_tile(seq, 512))
    bias_row = jnp.concatenate([b_f.astype(F32), jnp.zeros((LANES - N_FOX_HEADS,), F32)])[None, :]
    fcum = _forget_cumsum(small, bias_row, batch, seq)
    tq = min(ATT_BLOCK, seq)
    frow = jnp.transpose(fcum.reshape(batch, seq, LANES)[:, :, :N_FOX_HEADS], (0, 2, 1))
    frow = frow.reshape(batch, N_FOX_HEADS, seq // tq, tq)
    fox = _fox_attention(main, fcum, frow, batch, seq)
    dsa = _dsa_attention(main, small, batch, seq)
    wo = w_out.astype(BF16)
    zero_bias = jnp.zeros((1, D_MODEL), F32)
    return _matmul_ln([fox, dsa], [wo[:FOX_W], wo[FOX_W:]], zero_bias, x, g, b,
                      _row_tile(n, 512))


def kernel(x, norm_mix_g, norm_mix_b, norm_ffn_g, norm_ffn_b, attn_w_in, attn_b_f, attn_w_out,
           ffn_w_gate, ffn_w_up, ffn_w_down, conv_w_in, conv_b_in, conv_w_dw, conv_b_dw,
           conv_ln_g, conv_ln_b, conv_w_out, conv_b_out, moe_w_router, moe_w_gate, moe_w_up,
           moe_w_down):
    batch, seq, d = x.shape
    n = batch * seq
    tabs = tuple(jnp.tile(p, (1, MAIN_TN // HEAD_DIM)) for p in _rope_patterns(seq))
    r2 = lambda v: v.astype(F32).reshape(1, -1)

    xf = x.reshape(n, d)
    xb = xf.astype(BF16)
    for layer in range(DEPTH):
        j = layer // 2
        gm, bm = r2(norm_mix_g[layer]), r2(norm_mix_b[layer])
        gf, bf = r2(norm_ffn_g[layer]), r2(norm_ffn_b[layer])
        if layer % 2 == 0:
            xf, xb = _attention_layer(xf, xb, attn_w_in[j], attn_b_f[j], attn_w_out[j], gm, bm,
                                      tabs, batch, seq)
            xf, xb = _swiglu_ln(xb, xf, ffn_w_gate[j].astype(BF16), ffn_w_up[j].astype(BF16),
                                ffn_w_down[j].astype(BF16), gf, bf, _row_tile(n, 1024), 256)
        else:
            h = _glu_proj(xb, conv_w_in[j].astype(BF16), r2(conv_b_in[j]), _row_tile(n, 1024), 512)
            hb = _dwconv_ln_silu(h, conv_w_dw[j].astype(F32), r2(conv_b_dw[j]),
                                 r2(conv_ln_g[j]), r2(conv_ln_b[j]), batch, seq,
                                 min(256, seq))
            xf, xb = _matmul_ln([hb], [conv_w_out[j].astype(BF16)], r2(conv_b_out[j]), xf, gm, bm,
                                _row_tile(n, 512))
            w_r = jnp.concatenate([moe_w_router[j].astype(F32),
                                   jnp.zeros((d, LANES - N_EXPERTS), F32)], axis=1)
            xf, xb = _moe_ln(xb, xf, w_r, moe_w_gate[j].astype(BF16), moe_w_up[j].astype(BF16),
                             moe_w_down[j].astype(BF16), gf, bf, 512)
    return xf.reshape(batch, seq, d)
```

```python
import functools

import jax
import jax.numpy as jnp
from jax import lax
from jax.experimental import pallas as pl
from jax.experimental.pallas import tpu as pltpu
from jax.experimental.pallas import tpu_sc as plsc

F32 = jnp.float32
BF16 = jnp.bfloat16

D_MODEL = 1024
DEPTH = 4
HEAD_DIM = 64
FOX_W = 512
DSA_W = 512
N_FOX_HEADS = 8
N_IDX_HEADS = 4
IDX_DIM = 64
DSA_TOPK_MAX = 256
ROPE_DIM = 16
ROPE_THETA = 500000.0
CONV_WIDTH = 31
N_EXPERTS = 8
LN_EPS = 1e-5
ALPHA = (2.0 * DEPTH) ** 0.25

LANES = 128
NEG = -1e30
INT_MIN = -2 ** 31
KEY_NEG_INF = -2 ** 31 + 0x007FFFFF
VMEM_LIMIT = 56 * 1024 * 1024

ATT_BLOCK = 512
SEARCH_ROWS = 512
MAIN_COLS = 3328
MAIN_TN = 256
ROPE_TILES = (6, 7, 8, 9, 12)


def _cparams(sem):
    return pltpu.CompilerParams(dimension_semantics=sem, vmem_limit_bytes=VMEM_LIMIT)


def _layer_norm(y, g, b):
    mu = jnp.mean(y, axis=-1, keepdims=True)
    d = y - mu
    var = jnp.mean(d * d, axis=-1, keepdims=True)
    return d * lax.rsqrt(var + LN_EPS) * g + b


def _sigmoid(z):
    return 1.0 / (1.0 + jnp.exp(-z))


def _rope(acc, c, s1, s2):
    tn = acc.shape[1]
    return acc * c + pltpu.roll(acc, tn - 8, 1) * s1 + pltpu.roll(acc, 8, 1) * s2


def _proj_kernel(x_ref, w_ref, c_ref, s1_ref, s2_ref, main_ref, small_ref):
    x = x_ref[...]
    for j in range(MAIN_COLS // MAIN_TN):
        cols = slice(j * MAIN_TN, (j + 1) * MAIN_TN)
        acc = jnp.dot(x, w_ref[:, cols], preferred_element_type=F32)
        if j in ROPE_TILES:
            acc = _rope(acc, c_ref[...], s1_ref[...], s2_ref[...])
        main_ref[:, cols] = acc.astype(main_ref.dtype)
    acc = jnp.dot(x, w_ref[:, MAIN_COLS:MAIN_COLS + 2 * LANES], preferred_element_type=F32)
    small_ref[:, :LANES] = _rope(acc[:, :LANES], c_ref[:, :LANES], s1_ref[:, :LANES],
                                 s2_ref[:, :LANES])
    small_ref[:, LANES:] = acc[:, LANES:]


def _project(xb, w, tables, seq, tm):
    n, d = xb.shape
    pos_blocks = seq // tm
    tab_spec = pl.BlockSpec((tm, MAIN_TN), lambda i: (i % pos_blocks, 0))
    row = lambda i: (i, 0)
    return pl.pallas_call(
        _proj_kernel,
        grid=(n // tm,),
        in_specs=[pl.BlockSpec((tm, d), row),
                  pl.BlockSpec(w.shape, lambda i: (0, 0)),
                  tab_spec, tab_spec, tab_spec],
        out_specs=[pl.BlockSpec((tm, MAIN_COLS), row), pl.BlockSpec((tm, 2 * LANES), row)],
        out_shape=[jax.ShapeDtypeStruct((n, MAIN_COLS), BF16),
                   jax.ShapeDtypeStruct((n, 2 * LANES), F32)],
        compiler_params=_cparams(("parallel",)),
    )(xb, w, *tables)


def _forget_kernel(z_ref, b_ref, o_ref, *, chunk):
    s = z_ref.shape[0]
    z = z_ref[...] + b_ref[...]
    logf = jnp.minimum(z, 0.0) - jnp.log(1.0 + jnp.exp(-jnp.abs(z)))
    r = lax.broadcasted_iota(jnp.int32, (chunk, chunk), 0)
    c = lax.broadcasted_iota(jnp.int32, (chunk, chunk), 1)
    tri = jnp.where(c <= r, 1.0, 0.0).astype(F32)
    carry = jnp.zeros((1, LANES), F32)
    for k in range(s // chunk):
        blk = logf[k * chunk:(k + 1) * chunk, :]
        cs = jnp.dot(tri, blk, precision=lax.Precision.HIGHEST,
                     preferred_element_type=F32) + carry
        o_ref[k * chunk:(k + 1) * chunk, :] = cs
        carry = cs[chunk - 1:chunk, :]


def _forget_cumsum(small, bias_row, batch, seq):
    chunk = min(256, seq)
    return pl.pallas_call(
        functools.partial(_forget_kernel, chunk=chunk),
        grid=(batch,),
        in_specs=[pl.BlockSpec((seq, LANES), lambda b: (b, 1)),
                  pl.BlockSpec((1, LANES), lambda b: (0, 0))],
        out_specs=pl.BlockSpec((seq, LANES), lambda b: (b, 0)),
        out_shape=jax.ShapeDtypeStruct((batch * seq, LANES), F32),
        compiler_params=_cparams(("parallel",)),
    )(small, bias_row)


def _flash_pair(q2, load_k, load_v, bias_fn, n_plain, diag_chunk, s_scr, tq):
    lane = lax.broadcasted_iota(jnp.int32, (tq, LANES), 1)
    first = lane < HEAD_DIM
    qa = (jnp.where(first, q2, jnp.zeros_like(q2)), jnp.where(first, jnp.zeros_like(q2), q2))
    folds = s_scr.shape[3] // LANES

    def score(c, mx, diag):
        kc = load_k(c)
        new = []
        for a in range(2):
            s = lax.dot_general(qa[a], kc, (((1,), (1,)), ((), ())),
                                preferred_element_type=F32)
            s = bias_fn(a, c, s, diag)
            s_scr[a, c] = s
            m = mx[a]
            for f in range(folds):
                m = jnp.maximum(m, s[:, f * LANES:(f + 1) * LANES])
            new.append(m)
        return tuple(new)

    mx = tuple(jnp.full((tq, LANES), NEG, F32) for _ in range(2))
    mx = lax.fori_loop(0, n_plain, lambda c, m: score(c, m, False), mx)
    n_all = n_plain
    if diag_chunk is not None:
        mx = score(diag_chunk, mx, True)
        n_all = n_plain + 1
    row_max = [jnp.max(m, axis=1, keepdims=True) for m in mx]

    def weigh(c, carry):
        vc = load_v(c)
        new = []
        for a in range(2):
            l, acc = carry[a]
            p = jnp.exp(s_scr[a, c] - row_max[a])
            for f in range(folds):
                l = l + p[:, f * LANES:(f + 1) * LANES]
            acc = acc + jnp.dot(p.astype(BF16), vc, preferred_element_type=F32)
            new.append((l, acc))
        return tuple(new)

    init = tuple((jnp.zeros((tq, LANES), F32), jnp.zeros((tq, LANES), F32)) for _ in range(2))
    carry = lax.fori_loop(0, n_all, weigh, init)
    o0 = carry[0][1] / jnp.sum(carry[0][0], axis=1, keepdims=True)
    o1 = carry[1][1] / jnp.sum(carry[1][0], axis=1, keepdims=True)
    return jnp.where(first, o0, o1)


def _fox_kernel(q_ref, k_ref, v_ref, fc_ref, fr_ref, o_ref, s_scr, *, tq):
    hp = pl.program_id(1)
    i = pl.program_id(2)
    lane = lax.broadcasted_iota(jnp.int32, (tq, LANES), 1)
    fc = fc_ref[...]
    fq = [jnp.sum(jnp.where(lane == hp * 2 + a, fc, 0.0), axis=1, keepdims=True)
          for a in range(2)]
    row = lax.broadcasted_iota(jnp.int32, (tq, tq), 0)
    col = lax.broadcasted_iota(jnp.int32, (tq, tq), 1)

    def load_k(c):
        return k_ref[pl.ds(pl.multiple_of(c * tq, tq), tq), :]

    def load_v(c):
        return v_ref[pl.ds(pl.multiple_of(c * tq, tq), tq), :]

    def bias_fn(a, c, s, diag):
        fk = fr_ref[0, pl.ds(hp * 2 + a, 1), pl.ds(c, 1), :].reshape(1, tq)
        s = s + fq[a] - fk
        if diag:
            s = jnp.where(col <= row, s, NEG)
        return s

    out = _flash_pair(q_ref[...], load_k, load_v, bias_fn, i, i, s_scr, tq)
    o_ref[...] = out.astype(o_ref.dtype)


def _fox_attention(main, fcum, frow, batch, seq):
    tq = min(ATT_BLOCK, seq)
    nq = seq // tq
    n = batch * seq
    pairs = FOX_W // LANES
    return pl.pallas_call(
        functools.partial(_fox_kernel, tq=tq),
        grid=(batch, pairs, nq),
        in_specs=[pl.BlockSpec((tq, LANES), lambda b, h, i: (b * nq + i, h)),
                  pl.BlockSpec((seq, LANES), lambda b, h, i: (b, pairs + h)),
                  pl.BlockSpec((seq, LANES), lambda b, h, i: (b, 2 * pairs + h)),
                  pl.BlockSpec((tq, LANES), lambda b, h, i: (b * nq + i, 0)),
                  pl.BlockSpec((1, N_FOX_HEADS, nq, tq), lambda b, h, i: (b, 0, 0, 0))],
        out_specs=pl.BlockSpec((tq, LANES), lambda b, h, i: (b * nq + i, h)),
        out_shape=jax.ShapeDtypeStruct((n, FOX_W), BF16),
        scratch_shapes=[pltpu.VMEM((2, nq, tq, tq), F32)],
        compiler_params=_cparams(("parallel", "parallel", "arbitrary")),
    )(main, main, main, fcum, frow)


def _dsa_kernel(q_ref, k_ref, v_ref, iq_ref, ik_ref, iw_ref, o_ref,
                key_scr, bias_scr, kidx_scr, s_scr, *, tq, topk):
    i = pl.program_id(1)
    nch = i + 1

    @pl.when(i == 0)
    def _():
        kidx_scr[...] = ik_ref[...].astype(BF16)

    lane = lax.broadcasted_iota(jnp.int32, (tq, LANES), 1)
    first = lane < IDX_DIM
    row = lax.broadcasted_iota(jnp.int32, (tq, tq), 0)
    col = lax.broadcasted_iota(jnp.int32, (tq, tq), 1)
    iq = iq_ref[...]
    w = iw_ref[...] * ((N_IDX_HEADS * IDX_DIM) ** -0.5)
    qi, wh = [], []
    for h in range(N_IDX_HEADS):
        grp = iq[:, (h // 2) * LANES:(h // 2 + 1) * LANES]
        zero = jnp.zeros_like(grp)
        qi.append(jnp.where(first, grp, zero) if h % 2 == 0 else jnp.where(first, zero, grp))
        wh.append(w[:, N_FOX_HEADS + h:N_FOX_HEADS + h + 1])

    def score_chunk(c, diag):
        start = pl.multiple_of(c * tq, tq)
        kc = kidx_scr[pl.ds(start, tq), :]
        sc = jnp.zeros((tq, tq), F32)
        for h in range(N_IDX_HEADS):
            lg = lax.dot_general(qi[h], kc, (((1,), (1,)), ((), ())),
                                 preferred_element_type=F32)
            sc = sc + wh[h] * jnp.maximum(lg, 0.0)
        if diag:
            sc = jnp.where(col <= row, sc, -jnp.inf)
        sc = jnp.where(sc == 0.0, 0.0, sc)
        bits = pltpu.bitcast(sc, jnp.int32)
        key_scr[c] = bits ^ ((bits >> 31) & 0x7FFFFFFF)

    def plain_body(c, carry):
        score_chunk(c, False)
        return carry

    lax.fori_loop(0, i, plain_body, 0)
    score_chunk(i, True)

    kf = float(topk)
    rb = min(SEARCH_ROWS, tq)
    ones_col = jnp.ones((tq, LANES), BF16)
    ones_sq = jnp.ones((LANES, LANES), BF16)
    upper = jnp.where(row < col, 1.0, 0.0).astype(BF16)

    def wide(x):
        return jnp.concatenate([x] * (tq // LANES), axis=1)

    for r0 in range(0, tq, rb):
        rows = slice(r0, r0 + rb)

        def count(pred, rows=rows):
            def body(c, part):
                ones = jnp.where(pred(key_scr[c, rows, :]), 1.0, 0.0).astype(BF16)
                for g in range(tq // LANES):
                    part = part + ones[:, g * LANES:(g + 1) * LANES]
                return part
            part = lax.fori_loop(0, nch, body, jnp.zeros((rb, LANES), BF16))
            return jnp.dot(part, ones_sq, preferred_element_type=F32)

        def refine(cand, thr, count=count):
            cand_w = wide(cand)
            return jnp.where(count(lambda k: k >= cand_w) >= kf, cand, thr)

        thr = jnp.full((rb, LANES), INT_MIN, jnp.int32)
        thr = refine(jnp.zeros((rb, LANES), jnp.int32), thr)
        thr = lax.fori_loop(
            0, 31, lambda it, t: refine(t | jnp.left_shift(jnp.int32(1), 30 - it), t), thr)

        thr_w = wide(thr)
        need = kf - count(lambda k: k > thr_w)
        need_w = wide(jnp.where(thr == KEY_NEG_INF, 0.0, need))

        def sel_body(c, off, rows=rows, thr_w=thr_w, need_w=need_w):
            k = key_scr[c, rows, :]
            eq = k == thr_w
            eqb = jnp.where(eq, 1.0, 0.0).astype(BF16)
            before = jnp.dot(eqb, upper, preferred_element_type=F32) + wide(off)
            tie = jnp.where(before < need_w, 0.0, NEG)
            bias_scr[c, rows, :] = jnp.where(k > thr_w, 0.0, jnp.where(eq, tie, NEG))
            return off + jnp.dot(eqb, ones_col, preferred_element_type=F32)

        lax.fori_loop(0, nch, sel_body, jnp.zeros((rb, LANES), F32))

    for g in range(DSA_W // LANES):
        cols = slice(g * LANES, (g + 1) * LANES)

        def load_k(c, cols=cols):
            return k_ref[pl.ds(pl.multiple_of(c * tq, tq), tq), cols]

        def load_v(c, cols=cols):
            return v_ref[pl.ds(pl.multiple_of(c * tq, tq), tq), cols]

        def bias_fn(a, c, s, diag):
            return s + bias_scr[c]

        out = _flash_pair(q_ref[:, cols], load_k, load_v, bias_fn, nch, None, s_scr, tq)
        o_ref[:, cols] = out.astype(o_ref.dtype)


def _dsa_attention(main, small, batch, seq):
    tq = min(ATT_BLOCK, seq)
    nq = seq // tq
    n = batch * seq
    topk = min(DSA_TOPK_MAX, seq // 4)
    iq_block = MAIN_COLS // 256 - 1
    return pl.pallas_call(
        functools.partial(_dsa_kernel, tq=tq, topk=topk),
        grid=(batch, nq),
        in_specs=[pl.BlockSpec((tq, DSA_W), lambda b, i: (b * nq + i, 3)),
                  pl.BlockSpec((seq, DSA_W), lambda b, i: (b, 4)),
                  pl.BlockSpec((seq, DSA_W), lambda b, i: (b, 5)),
                  pl.BlockSpec((tq, 256), lambda b, i: (b * nq + i, iq_block)),
                  pl.BlockSpec((seq, LANES), lambda b, i: (b, 0)),
                  pl.BlockSpec((tq, LANES), lambda b, i: (b * nq + i, 1))],
        out_specs=pl.BlockSpec((tq, DSA_W), lambda b, i: (b * nq + i, 0)),
        out_shape=jax.ShapeDtypeStruct((n, DSA_W), BF16),
        scratch_shapes=[pltpu.VMEM((nq, tq, tq), jnp.int32),
                        pltpu.VMEM((nq, tq, tq), F32),
                        pltpu.VMEM((seq, LANES), BF16),
                        pltpu.VMEM((2, nq, tq, tq), F32)],
        compiler_params=_cparams(("parallel", "arbitrary")),
    )(main, main, main, main, small, small)


def _matmul_ln_kernel(*refs, n_pairs):
    a_refs = refs[:n_pairs]
    w_refs = refs[n_pairs:2 * n_pairs]
    bias_ref, x_ref, g_ref, b_ref, o_ref, ob_ref = refs[2 * n_pairs:]
    acc = bias_ref[...] + ALPHA * x_ref[...]
    for a_ref, w_ref in zip(a_refs, w_refs):
        acc = acc + jnp.dot(a_ref[...], w_ref[...], preferred_element_type=F32)
    y = _layer_norm(acc, g_ref[...], b_ref[...])
    o_ref[...] = y
    ob_ref[...] = y.astype(BF16)


def _matmul_ln(a_list, w_list, bias, x, g, b, tm):
    n, d = x.shape
    n_pairs = len(a_list)
    row = lambda i: (i, 0)
    fixed = lambda i: (0, 0)
    in_specs = ([pl.BlockSpec((tm, a.shape[1]), row) for a in a_list]
                + [pl.BlockSpec(w.shape, fixed) for w in w_list]
                + [pl.BlockSpec((1, d), fixed), pl.BlockSpec((tm, d), row),
                   pl.BlockSpec((1, d), fixed), pl.BlockSpec((1, d), fixed)])
    return pl.pallas_call(
        functools.partial(_matmul_ln_kernel, n_pairs=n_pairs),
        grid=(n // tm,),
        in_specs=in_specs,
        out_specs=[pl.BlockSpec((tm, d), row), pl.BlockSpec((tm, d), row)],
        out_shape=[jax.ShapeDtypeStruct((n, d), F32), jax.ShapeDtypeStruct((n, d), BF16)],
        compiler_params=_cparams(("parallel",)),
    )(*a_list, *w_list, bias, x, g, b)


def _swiglu_kernel(xb_ref, wg_ref, wu_ref, wd_ref, x_ref, g_ref, b_ref, o_ref, ob_ref, acc_ref):
    f = pl.program_id(1)

    @pl.when(f == 0)
    def _():
        acc_ref[...] = ALPHA * x_ref[...]

    xb = xb_ref[...]
    hg = jnp.dot(xb, wg_ref[...], preferred_element_type=F32)
    hu = jnp.dot(xb, wu_ref[...], preferred_element_type=F32)
    h = hg * _sigmoid(hg) * hu
    acc_ref[...] += jnp.dot(h.astype(BF16), wd_ref[...], preferred_element_type=F32)

    @pl.when(f == pl.num_programs(1) - 1)
    def _():
        y = _layer_norm(acc_ref[...], g_ref[...], b_ref[...])
        o_ref[...] = y
        ob_ref[...] = y.astype(BF16)


def _swiglu_ln(xb, x, wg, wu, wd, g, b, tm, tf):
    n, d = x.shape
    ffn = wg.shape[1]
    row = lambda i, f: (i, 0)
    fixed = lambda i, f: (0, 0)
    return pl.pallas_call(
        _swiglu_kernel,
        grid=(n // tm, ffn // tf),
        in_specs=[pl.BlockSpec((tm, d), row),
                  pl.BlockSpec((d, tf), lambda i, f: (0, f)),
                  pl.BlockSpec((d, tf), lambda i, f: (0, f)),
                  pl.BlockSpec((tf, d), lambda i, f: (f, 0)),
                  pl.BlockSpec((tm, d), row),
                  pl.BlockSpec((1, d), fixed), pl.BlockSpec((1, d), fixed)],
        out_specs=[pl.BlockSpec((tm, d), row), pl.BlockSpec((tm, d), row)],
        out_shape=[jax.ShapeDtypeStruct((n, d), F32), jax.ShapeDtypeStruct((n, d), BF16)],
        scratch_shapes=[pltpu.VMEM((tm, d), F32)],
        compiler_params=_cparams(("parallel", "arbitrary")),
    )(xb, wg, wu, wd, x, g, b)


def _glu_kernel(x_ref, wa_ref, wg_ref, ba_ref, bg_ref, o_ref):
    xb = x_ref[...]
    a = jnp.dot(xb, wa_ref[...], preferred_element_type=F32) + ba_ref[...]
    gate = jnp.dot(xb, wg_ref[...], preferred_element_type=F32) + bg_ref[...]
    o_ref[...] = a * _sigmoid(gate)


def _glu_proj(xb, w, bias, tm, tn):
    n, d = xb.shape
    c = w.shape[1] // 2
    nc = c // tn
    return pl.pallas_call(
        _glu_kernel,
        grid=(n // tm, nc),
        in_specs=[pl.BlockSpec((tm, d), lambda i, j: (i, 0)),
                  pl.BlockSpec((d, tn), lambda i, j: (0, j)),
                  pl.BlockSpec((d, tn), lambda i, j: (0, nc + j)),
                  pl.BlockSpec((1, tn), lambda i, j: (0, j)),
                  pl.BlockSpec((1, tn), lambda i, j: (0, nc + j))],
        out_specs=pl.BlockSpec((tm, tn), lambda i, j: (i, j)),
        out_shape=jax.ShapeDtypeStruct((n, c), F32),
        compiler_params=_cparams(("parallel", "arbitrary")),
    )(xb, w, w, bias, bias)


def _dwconv_kernel(cur_ref, prev_ref, w_ref, bdw_ref, g_ref, b_ref, o_ref, buf_ref, sh_ref, y_ref,
                   *, ts, halo):
    i = pl.program_id(1)
    prev = prev_ref[...]
    buf_ref[0:halo, :] = jnp.where(i == 0, jnp.zeros_like(prev), prev)
    buf_ref[halo:halo + ts, :] = cur_ref[...]
    base = halo - (CONV_WIDTH - 1)
    span = sh_ref.shape[1]
    for cb in range(cur_ref.shape[1] // LANES):
        cols = slice(cb * LANES, (cb + 1) * LANES)
        for r in range(1, 8):
            sh_ref[r - 1] = buf_ref[r:r + span, cols]
        acc = jnp.zeros((ts, LANES), F32)
        for j in range(CONV_WIDTH):
            a, r = divmod(base + j, 8)
            if r == 0:
                win = buf_ref[8 * a:8 * a + ts, cols]
            else:
                win = sh_ref[r - 1, 8 * a:8 * a + ts, :]
            acc = acc + w_ref[j:j + 1, cols] * win
        y_ref[:, cols] = acc
    y = _layer_norm(y_ref[...] + bdw_ref[...], g_ref[...], b_ref[...])
    o_ref[...] = (y * _sigmoid(y)).astype(o_ref.dtype)


def _dwconv_ln_silu(h, w_dw, b_dw, g, b, batch, seq, ts):
    n, c = h.shape
    halo = 32
    nt = seq // ts
    ratio = ts // halo
    fixed = lambda bi, i: (0, 0)
    cur = lambda bi, i: (bi * nt + i, 0)
    prev = lambda bi, i: (jnp.maximum((bi * nt + i) * ratio - 1, 0), 0)
    return pl.pallas_call(
        functools.partial(_dwconv_kernel, ts=ts, halo=halo),
        grid=(batch, nt),
        in_specs=[pl.BlockSpec((ts, c), cur), pl.BlockSpec((halo, c), prev),
                  pl.BlockSpec((CONV_WIDTH, c), fixed),
                  pl.BlockSpec((1, c), fixed), pl.BlockSpec((1, c), fixed),
                  pl.BlockSpec((1, c), fixed)],
        out_specs=pl.BlockSpec((ts, c), cur),
        out_shape=jax.ShapeDtypeStruct((n, c), BF16),
        scratch_shapes=[pltpu.VMEM((halo + ts, c), F32),
                        pltpu.VMEM((7, ts + halo - 8, LANES), F32),
                        pltpu.VMEM((ts, c), F32)],
        compiler_params=_cparams(("parallel", "arbitrary")),
    )(h, h, w_dw, b_dw, g, b)


MOE_TILE = 512
MOE_EXPERT_TILE = 1024
SC_CORES = 2
SC_SUBCORES = 16
SC_GATHER_CHUNK = 64


def _router_kernel(x_ref, w_ref, info_ref, cum_ref, count_scr):
    @pl.when(pl.program_id(0) == 0)
    def _():
        count_scr[...] = jnp.zeros_like(count_scr)

    logits = jnp.dot(x_ref[...], w_ref[...], precision=lax.Precision.HIGHEST,
                     preferred_element_type=F32)
    tm = logits.shape[0]
    lane = lax.broadcasted_iota(jnp.int32, logits.shape, 1)
    logits = jnp.where(lane < N_EXPERTS, logits, -jnp.inf)
    m1 = jnp.max(logits, axis=1, keepdims=True)
    i1 = jnp.min(jnp.where(logits == m1, lane, LANES), axis=1, keepdims=True)
    rest = jnp.where(lane == i1, -jnp.inf, logits)
    m2 = jnp.max(rest, axis=1, keepdims=True)
    i2 = jnp.min(jnp.where(rest == m2, lane, LANES), axis=1, keepdims=True)
    e2 = jnp.exp(m2 - m1)
    g1 = 1.0 / (1.0 + e2)
    g2 = e2 / (1.0 + e2)
    routed = jnp.where(lane == i1, 1.0, jnp.where(lane == i2, 1.0, 0.0))
    r = lax.broadcasted_iota(jnp.int32, (tm, tm), 0)
    c = lax.broadcasted_iota(jnp.int32, (tm, tm), 1)
    earlier = jnp.where(c < r, 1.0, 0.0).astype(BF16)
    before = jnp.dot(earlier, routed.astype(BF16), preferred_element_type=F32)
    start = count_scr[0:1, :]
    rank = before + start
    r1 = jnp.sum(jnp.where(lane == i1, rank, 0.0), axis=1, keepdims=True)
    r2 = jnp.sum(jnp.where(lane == i2, rank, 0.0), axis=1, keepdims=True)
    fields = (i1.astype(F32), i2.astype(F32), r1, r2, g1, g2)
    info = jnp.zeros(logits.shape, F32)
    for k, v in enumerate(fields):
        info = jnp.where(lane == k, v, info)
    info_ref[...] = info
    after = start + before[tm - 1:tm, :] + routed[tm - 1:tm, :]
    count_scr[...] = jnp.broadcast_to(after, count_scr.shape)
    cum_ref[...] = jnp.broadcast_to(after, cum_ref.shape)


def _router(x, w_router_padded, tm):
    n, d = x.shape
    row = lambda i: (i, 0)
    return pl.pallas_call(
        _router_kernel,
        grid=(n // tm,),
        in_specs=[pl.BlockSpec((tm, d), row),
                  pl.BlockSpec((d, LANES), lambda i: (0, 0))],
        out_specs=[pl.BlockSpec((tm, LANES), row), pl.BlockSpec((8, LANES), row)],
        out_shape=[jax.ShapeDtypeStruct((n, LANES), F32),
                   jax.ShapeDtypeStruct((n // tm * 8, LANES), F32)],
        scratch_shapes=[pltpu.VMEM((8, LANES), F32)],
        compiler_params=_cparams(("arbitrary",)),
    )(x, w_router_padded)


def _gather_rows(table, idx):
    n_idx = idx.shape[0]
    width = table.shape[1]
    per_worker = n_idx // (SC_CORES * SC_SUBCORES)
    chunk = SC_GATHER_CHUNK
    assert per_worker * SC_CORES * SC_SUBCORES == n_idx and per_worker % chunk == 0
    mesh = plsc.VectorSubcoreMesh(core_axis_name="c", subcore_axis_name="s")

    def body(table_hbm, idx_hbm, out_hbm, idx_v, rows_v, sem):
        worker = lax.axis_index("s") * SC_CORES + lax.axis_index("c")
        base = worker * per_worker

        @pl.loop(0, per_worker // chunk)
        def _(j):
            off = pl.multiple_of(base + j * chunk, chunk)
            pltpu.sync_copy(idx_hbm.at[pl.ds(off, chunk)], idx_v)
            pltpu.async_copy(table_hbm.at[idx_v], rows_v, sem).wait()
            pltpu.sync_copy(rows_v, out_hbm.at[pl.ds(off, chunk)])

    return pl.kernel(
        body, out_type=jax.ShapeDtypeStruct((n_idx, width), table.dtype), mesh=mesh,
        scratch_types=[pltpu.VMEM((chunk,), jnp.int32), pltpu.VMEM((chunk, width), table.dtype),
                       pltpu.SemaphoreType.DMA],
    )(table, idx)


def _as_words(a):
    rows, cols = a.shape
    return lax.bitcast_convert_type(a.reshape(rows, cols // 2, 2), jnp.int32)


def _as_bf16(w):
    return lax.bitcast_convert_type(w, BF16).reshape(w.shape[0], -1)


def _expert_kernel(e_of_big, nvalid, xg_ref, wg_ref, wu_ref, wd_ref, y_ref, acc_ref):
    b = pl.program_id(0)
    f = pl.program_id(1)
    nv = nvalid[b]

    @pl.when(nv > 0)
    def _():
        @pl.when(f == 0)
        def _():
            acc_ref[...] = jnp.zeros_like(acc_ref)

        xg = xg_ref[...]
        hg = jnp.dot(xg, wg_ref[...], preferred_element_type=F32)
        hu = jnp.dot(xg, wu_ref[...], preferred_element_type=F32)
        h = hg * _sigmoid(hg) * hu
        acc_ref[...] += jnp.dot(h.astype(BF16), wd_ref[...], preferred_element_type=F32)

        @pl.when(f == pl.num_programs(1) - 1)
        def _():
            y_ref[...] = acc_ref[...].astype(y_ref.dtype)


def _expert_ffn(e_of_big, nvalid, xg, wg, wu, wd, tf):
    n_slots, d = xg.shape
    edim = wg.shape[2]
    tm = MOE_EXPERT_TILE
    row = lambda b, f, e, nv: (b, 0)
    return pl.pallas_call(
        _expert_kernel,
        grid_spec=pltpu.PrefetchScalarGridSpec(
            num_scalar_prefetch=2,
            grid=(n_slots // tm, edim // tf),
            in_specs=[pl.BlockSpec((tm, d), row),
                      pl.BlockSpec((None, d, tf), lambda b, f, e, nv: (e[b], 0, f)),
                      pl.BlockSpec((None, d, tf), lambda b, f, e, nv: (e[b], 0, f)),
                      pl.BlockSpec((None, tf, d), lambda b, f, e, nv: (e[b], f, 0))],
            out_specs=pl.BlockSpec((tm, d), row),
            scratch_shapes=[pltpu.VMEM((tm, d), F32)]),
        out_shape=jax.ShapeDtypeStruct((n_slots, d), BF16),
        compiler_params=_cparams(("parallel", "arbitrary")),
    )(e_of_big, nvalid, xg, wg, wu, wd)


def _combine_kernel(info_ref, y1_ref, y2_ref, x_ref, g_ref, b_ref, o_ref, ob_ref):
    info = info_ref[...]
    ff = (info[:, 4:5] * y1_ref[...].astype(F32) + info[:, 5:6] * y2_ref[...].astype(F32))
    y = _layer_norm(ALPHA * x_ref[...] + ff, g_ref[...], b_ref[...])
    o_ref[...] = y
    ob_ref[...] = y.astype(BF16)


def _combine_ln(info, yy, x, g, b, tm):
    n, d = x.shape
    nt = n // tm
    row = lambda i: (i, 0)
    fixed = lambda i: (0, 0)
    return pl.pallas_call(
        _combine_kernel,
        grid=(nt,),
        in_specs=[pl.BlockSpec((tm, LANES), row),
                  pl.BlockSpec((tm, d), row), pl.BlockSpec((tm, d), lambda i: (nt + i, 0)),
                  pl.BlockSpec((tm, d), row),
                  pl.BlockSpec((1, d), fixed), pl.BlockSpec((1, d), fixed)],
        out_specs=[pl.BlockSpec((tm, d), row), pl.BlockSpec((tm, d), row)],
        out_shape=[jax.ShapeDtypeStruct((n, d), F32), jax.ShapeDtypeStruct((n, d), BF16)],
        compiler_params=_cparams(("parallel",)),
    )(info, yy, yy, x, g, b)


def _routing_tables(info, cum, n):
    te = MOE_EXPERT_TILE
    i32 = jnp.int32
    totals = cum[-1, :N_EXPERTS].astype(i32)
    padded = (totals + te - 1) // te * te
    off = jnp.cumsum(padded) - padded
    n_slots = 2 * n + N_EXPERTS * te
    big_start = jnp.arange(n_slots // te, dtype=i32) * te
    e_of_big = jnp.minimum(jnp.sum(big_start[:, None] >= (off + padded)[None, :], axis=1),
                           N_EXPERTS - 1).astype(i32)
    nvalid = jnp.clip(totals[e_of_big] - (big_start - off[e_of_big]), 0, te).astype(i32)

    e1, e2 = info[:, 0].astype(i32), info[:, 1].astype(i32)
    slot1 = off[e1] + info[:, 2].astype(i32)
    slot2 = off[e2] + info[:, 3].astype(i32)
    tokens = jnp.arange(n, dtype=i32)
    tok_of_slot = jnp.zeros((n_slots,), i32).at[slot1].set(tokens).at[slot2].set(tokens)
    return e_of_big, nvalid, tok_of_slot, jnp.concatenate([slot1, slot2])


def _moe_ln(xb, x, w_router_padded, wg, wu, wd, g, b, tf):
    n, _ = x.shape
    info, cum = _router(x, w_router_padded, MOE_TILE)
    e_of_big, nvalid, tok_of_slot, slot_of_tok = _routing_tables(info, cum, n)
    xg = _as_bf16(_gather_rows(_as_words(xb), tok_of_slot))
    y = _expert_ffn(e_of_big, nvalid, xg, wg, wu, wd, tf)
    yy = _as_bf16(_gather_rows(_as_words(y), slot_of_tok))
    return _combine_ln(info, yy, x, g, b, MOE_TILE)


def _rope_patterns(seq):
    half = ROPE_DIM // 2
    inv_freq = ROPE_THETA ** (-jnp.arange(0, ROPE_DIM, 2, dtype=F32) / ROPE_DIM)
    ang = jnp.arange(seq, dtype=F32)[:, None] * inv_freq[None, :]
    cos, sin = jnp.cos(ang), jnp.sin(ang)
    rest = HEAD_DIM - ROPE_DIM
    c = jnp.concatenate([cos, cos, jnp.ones((seq, rest), F32)], axis=1)
    s1 = jnp.concatenate([-sin, jnp.zeros((seq, half + rest), F32)], axis=1)
    s2 = jnp.concatenate([jnp.zeros((seq, half), F32), sin, jnp.zeros((seq, rest), F32)], axis=1)
    return c, s1, s2


def _row_tile(n, want):
    return want if n % want == 0 else n


def _attention_layer(x, xb, w_in, b_f, w_out, g, b, tabs, batch, seq):
    n = batch * seq
    o = [0, 512, 1024, 1536, 1544, 2056, 2568, 3080, 3336, 3400, 3404]
    part = lambda k: w_in[:, o[k]:o[k + 1]]
    scale = HEAD_DIM ** -0.5
    pad = jnp.zeros((D_MODEL, LANES - N_FOX_HEADS - N_IDX_HEADS), F32)
    w_all = jnp.concatenate([part(0) * scale, part(1), part(2), part(4) * scale, part(5),
                             part(6), part(7), part(8), part(8), part(3), part(9), pad],
                            axis=1).astype(BF16)
    main, small = _project(xb, w_all, tabs, seq, _row_tile(seq, 512))
    bias_row = jnp.concatenate([b_f.astype(F32), jnp.zeros((LANES - N_FOX_HEADS,), F32)])[None, :]
    fcum = _forget_cumsum(small, bias_row, batch, seq)
    tq = min(ATT_BLOCK, seq)
    frow = jnp.transpose(fcum.reshape(batch, seq, LANES)[:, :, :N_FOX_HEADS], (0, 2, 1))
    frow = frow.reshape(batch, N_FOX_HEADS, seq // tq, tq)
    fox = _fox_attention(main, fcum, frow, batch, seq)
    dsa = _dsa_attention(main, small, batch, seq)
    wo = w_out.astype(BF16)
    zero_bias = jnp.zeros((1, D_MODEL), F32)
    return _matmul_ln([fox, dsa], [wo[:FOX_W], wo[FOX_W:]], zero_bias, x, g, b,
                      _row_tile(n, 512))


def kernel(x, norm_mix_g, norm_mix_b, norm_ffn_g, norm_ffn_b, attn_w_in, attn_b_f, attn_w_out,
           ffn_w_gate, ffn_w_up, ffn_w_down, conv_w_in, conv_b_in, conv_w_dw, conv_b_dw,
           conv_ln_g, conv_ln_b, conv_w_out, conv_b_out, moe_w_router, moe_w_gate, moe_w_up,
           moe_w_down):
    batch, seq, d = x.shape
    n = batch * seq
    tabs = tuple(jnp.tile(p, (1, MAIN_TN // HEAD_DIM)) for p in _rope_patterns(seq))
    r2 = lambda v: v.astype(F32).reshape(1, -1)

    xf = x.reshape(n, d)
    xb = xf.astype(BF16)
    for layer in range(DEPTH):
        j = layer // 2
        gm, bm = r2(norm_mix_g[layer]), r2(norm_mix_b[layer])
        gf, bf = r2(norm_ffn_g[layer]), r2(norm_ffn_b[layer])
        if layer % 2 == 0:
            xf, xb = _attention_layer(xf, xb, attn_w_in[j], attn_b_f[j], attn_w_out[j], gm, bm,
                                      tabs, batch, seq)
            xf, xb = _swiglu_ln(xb, xf, ffn_w_gate[j].astype(BF16), ffn_w_up[j].astype(BF16),
                                ffn_w_down[j].astype(BF16), gf, bf, _row_tile(n, 1024), 256)
        else:
            h = _glu_proj(xb, conv_w_in[j].astype(BF16), r2(conv_b_in[j]), _row_tile(n, 1024), 512)
            hb = _dwconv_ln_silu(h, conv_w_dw[j].astype(F32), r2(conv_b_dw[j]),
                                 r2(conv_ln_g[j]), r2(conv_ln_b[j]), batch, seq,
                                 min(256, seq))
            xf, xb = _matmul_ln([hb], [conv_w_out[j].astype(BF16)], r2(conv_b_out[j]), xf, gm, bm,
                                _row_tile(n, 512))
            w_r = jnp.concatenate([moe_w_router[j].astype(F32),
                                   jnp.zeros((d, LANES - N_EXPERTS), F32)], axis=1)
            xf, xb = _moe_ln(xb, xf, w_r, moe_w_gate[j].astype(BF16), moe_w_up[j].astype(BF16),
                             moe_w_down[j].astype(BF16), gf, bf, 512)
    return xf.reshape(batch, seq, d)
```

```python
import functools

import jax
import jax.numpy as jnp
from jax import lax
from jax.experimental import pallas as pl
from jax.experimental.pallas import tpu as pltpu

F32 = jnp.float32
BF16 = jnp.bfloat16

D_MODEL = 1024
DEPTH = 4
HEAD_DIM = 64
FOX_W = 512
DSA_W = 512
N_FOX_HEADS = 8
N_IDX_HEADS = 4
IDX_DIM = 64
DSA_TOPK_MAX = 256
ROPE_DIM = 16
ROPE_THETA = 500000.0
CONV_WIDTH = 31
N_EXPERTS = 8
LN_EPS = 1e-5
ALPHA = (2.0 * DEPTH) ** 0.25

LANES = 128
NEG = -1e30
INT_MIN = -2 ** 31
KEY_NEG_INF = -2 ** 31 + 0x007FFFFF
VMEM_LIMIT = 56 * 1024 * 1024

ATT_BLOCK = 512
MAIN_COLS = 3328
MAIN_TN = 256
ROPE_TILES = (6, 7, 8, 9, 12)


def _cparams(sem):
    return pltpu.CompilerParams(dimension_semantics=sem, vmem_limit_bytes=VMEM_LIMIT)


def _layer_norm(y, g, b):
    mu = jnp.mean(y, axis=-1, keepdims=True)
    d = y - mu
    var = jnp.mean(d * d, axis=-1, keepdims=True)
    return d * lax.rsqrt(var + LN_EPS) * g + b


def _sigmoid(z):
    return 1.0 / (1.0 + jnp.exp(-z))


def _rope(acc, c, s1, s2):
    tn = acc.shape[1]
    return acc * c + pltpu.roll(acc, tn - 8, 1) * s1 + pltpu.roll(acc, 8, 1) * s2


def _proj_kernel(x_ref, w_ref, c_ref, s1_ref, s2_ref, main_ref, small_ref):
    x = x_ref[...]
    for j in range(MAIN_COLS // MAIN_TN):
        cols = slice(j * MAIN_TN, (j + 1) * MAIN_TN)
        acc = jnp.dot(x, w_ref[:, cols], preferred_element_type=F32)
        if j in ROPE_TILES:
            acc = _rope(acc, c_ref[...], s1_ref[...], s2_ref[...])
        main_ref[:, cols] = acc.astype(main_ref.dtype)
    acc = jnp.dot(x, w_ref[:, MAIN_COLS:MAIN_COLS + 2 * LANES], preferred_element_type=F32)
    small_ref[:, :LANES] = _rope(acc[:, :LANES], c_ref[:, :LANES], s1_ref[:, :LANES],
                                 s2_ref[:, :LANES])
    small_ref[:, LANES:] = acc[:, LANES:]


def _project(xb, w, tables, seq, tm):
    n, d = xb.shape
    pos_blocks = seq // tm
    tab_spec = pl.BlockSpec((tm, MAIN_TN), lambda i: (i % pos_blocks, 0))
    row = lambda i: (i, 0)
    return pl.pallas_call(
        _proj_kernel,
        grid=(n // tm,),
        in_specs=[pl.BlockSpec((tm, d), row),
                  pl.BlockSpec(w.shape, lambda i: (0, 0)),
                  tab_spec, tab_spec, tab_spec],
        out_specs=[pl.BlockSpec((tm, MAIN_COLS), row), pl.BlockSpec((tm, 2 * LANES), row)],
        out_shape=[jax.ShapeDtypeStruct((n, MAIN_COLS), BF16),
                   jax.ShapeDtypeStruct((n, 2 * LANES), F32)],
        compiler_params=_cparams(("parallel",)),
    )(xb, w, *tables)


def _forget_kernel(z_ref, b_ref, o_ref, *, chunk):
    s = z_ref.shape[0]
    z = z_ref[...] + b_ref[...]
    logf = jnp.minimum(z, 0.0) - jnp.log(1.0 + jnp.exp(-jnp.abs(z)))
    r = lax.broadcasted_iota(jnp.int32, (chunk, chunk), 0)
    c = lax.broadcasted_iota(jnp.int32, (chunk, chunk), 1)
    tri = jnp.where(c <= r, 1.0, 0.0).astype(F32)
    carry = jnp.zeros((1, LANES), F32)
    for k in range(s // chunk):
        blk = logf[k * chunk:(k + 1) * chunk, :]
        cs = jnp.dot(tri, blk, precision=lax.Precision.HIGHEST,
                     preferred_element_type=F32) + carry
        o_ref[k * chunk:(k + 1) * chunk, :] = cs
        carry = cs[chunk - 1:chunk, :]


def _forget_cumsum(small, bias_row, batch, seq):
    chunk = min(256, seq)
    return pl.pallas_call(
        functools.partial(_forget_kernel, chunk=chunk),
        grid=(batch,),
        in_specs=[pl.BlockSpec((seq, LANES), lambda b: (b, 1)),
                  pl.BlockSpec((1, LANES), lambda b: (0, 0))],
        out_specs=pl.BlockSpec((seq, LANES), lambda b: (b, 0)),
        out_shape=jax.ShapeDtypeStruct((batch * seq, LANES), F32),
        compiler_params=_cparams(("parallel",)),
    )(small, bias_row)


def _flash_pair(q2, load_k, load_v, bias_fn, n_plain, diag_chunk, s_scr, tq):
    lane = lax.broadcasted_iota(jnp.int32, (tq, LANES), 1)
    first = lane < HEAD_DIM
    qa = (jnp.where(first, q2, jnp.zeros_like(q2)), jnp.where(first, jnp.zeros_like(q2), q2))
    folds = s_scr.shape[3] // LANES

    def score(c, mx, diag):
        kc = load_k(c)
        new = []
        for a in range(2):
            s = lax.dot_general(qa[a], kc, (((1,), (1,)), ((), ())),
                                preferred_element_type=F32)
            s = bias_fn(a, c, s, diag)
            s_scr[a, c] = s
            m = mx[a]
            for f in range(folds):
                m = jnp.maximum(m, s[:, f * LANES:(f + 1) * LANES])
            new.append(m)
        return tuple(new)

    mx = tuple(jnp.full((tq, LANES), NEG, F32) for _ in range(2))
    mx = lax.fori_loop(0, n_plain, lambda c, m: score(c, m, False), mx)
    n_all = n_plain
    if diag_chunk is not None:
        mx = score(diag_chunk, mx, True)
        n_all = n_plain + 1
    row_max = [jnp.max(m, axis=1, keepdims=True) for m in mx]

    def weigh(c, carry):
        vc = load_v(c)
        new = []
        for a in range(2):
            l, acc = carry[a]
            p = jnp.exp(s_scr[a, c] - row_max[a])
            for f in range(folds):
                l = l + p[:, f * LANES:(f + 1) * LANES]
            acc = acc + jnp.dot(p.astype(BF16), vc, preferred_element_type=F32)
            new.append((l, acc))
        return tuple(new)

    init = tuple((jnp.zeros((tq, LANES), F32), jnp.zeros((tq, LANES), F32)) for _ in range(2))
    carry = lax.fori_loop(0, n_all, weigh, init)
    o0 = carry[0][1] / jnp.sum(carry[0][0], axis=1, keepdims=True)
    o1 = carry[1][1] / jnp.sum(carry[1][0], axis=1, keepdims=True)
    return jnp.where(first, o0, o1)


def _fox_kernel(q_ref, k_ref, v_ref, fc_ref, fr_ref, o_ref, s_scr, *, tq):
    hp = pl.program_id(1)
    i = pl.program_id(2)
    lane = lax.broadcasted_iota(jnp.int32, (tq, LANES), 1)
    fc = fc_ref[...]
    fq = [jnp.sum(jnp.where(lane == hp * 2 + a, fc, 0.0), axis=1, keepdims=True)
          for a in range(2)]
    row = lax.broadcasted_iota(jnp.int32, (tq, tq), 0)
    col = lax.broadcasted_iota(jnp.int32, (tq, tq), 1)

    def load_k(c):
        return k_ref[pl.ds(pl.multiple_of(c * tq, tq), tq), :]

    def load_v(c):
        return v_ref[pl.ds(pl.multiple_of(c * tq, tq), tq), :]

    def bias_fn(a, c, s, diag):
        fk = fr_ref[0, pl.ds(hp * 2 + a, 1), pl.ds(c, 1), :].reshape(1, tq)
        s = s + fq[a] - fk
        if diag:
            s = jnp.where(col <= row, s, NEG)
        return s

    out = _flash_pair(q_ref[...], load_k, load_v, bias_fn, i, i, s_scr, tq)
    o_ref[...] = out.astype(o_ref.dtype)


def _fox_attention(main, fcum, frow, batch, seq):
    tq = min(ATT_BLOCK, seq)
    nq = seq // tq
    n = batch * seq
    pairs = FOX_W // LANES
    return pl.pallas_call(
        functools.partial(_fox_kernel, tq=tq),
        grid=(batch, pairs, nq),
        in_specs=[pl.BlockSpec((tq, LANES), lambda b, h, i: (b * nq + i, h)),
                  pl.BlockSpec((seq, LANES), lambda b, h, i: (b, pairs + h)),
                  pl.BlockSpec((seq, LANES), lambda b, h, i: (b, 2 * pairs + h)),
                  pl.BlockSpec((tq, LANES), lambda b, h, i: (b * nq + i, 0)),
                  pl.BlockSpec((1, N_FOX_HEADS, nq, tq), lambda b, h, i: (b, 0, 0, 0))],
        out_specs=pl.BlockSpec((tq, LANES), lambda b, h, i: (b * nq + i, h)),
        out_shape=jax.ShapeDtypeStruct((n, FOX_W), BF16),
        scratch_shapes=[pltpu.VMEM((2, nq, tq, tq), F32)],
        compiler_params=_cparams(("parallel", "parallel", "arbitrary")),
    )(main, main, main, fcum, frow)


def _dsa_kernel(q_ref, k_ref, v_ref, iq_ref, ik_ref, iw_ref, o_ref,
                key_scr, bias_scr, kidx_scr, s_scr, *, tq, topk):
    i = pl.program_id(1)
    nch = i + 1

    @pl.when(i == 0)
    def _():
        kidx_scr[...] = ik_ref[...].astype(BF16)

    lane = lax.broadcasted_iota(jnp.int32, (tq, LANES), 1)
    first = lane < IDX_DIM
    row = lax.broadcasted_iota(jnp.int32, (tq, tq), 0)
    col = lax.broadcasted_iota(jnp.int32, (tq, tq), 1)
    iq = iq_ref[...]
    w = iw_ref[...] * ((N_IDX_HEADS * IDX_DIM) ** -0.5)
    qi = []
    for h in range(N_IDX_HEADS):
        grp = iq[:, (h // 2) * LANES:(h // 2 + 1) * LANES]
        zero = jnp.zeros_like(grp)
        qi.append(jnp.where(first, grp, zero) if h % 2 == 0 else jnp.where(first, zero, grp))

    w_t = jnp.transpose(w)
    wh = [w_t[N_FOX_HEADS + h:N_FOX_HEADS + h + 1, :] for h in range(N_IDX_HEADS)]

    def score_chunk(c, diag):
        start = pl.multiple_of(c * tq, tq)
        kc = kidx_scr[pl.ds(start, tq), :]
        sc = jnp.zeros((tq, tq), F32)
        for h in range(N_IDX_HEADS):
            lg = lax.dot_general(kc, qi[h], (((1,), (1,)), ((), ())),
                                 preferred_element_type=F32)
            sc = sc + wh[h] * jnp.maximum(lg, 0.0)
        if diag:
            sc = jnp.where(row <= col, sc, -jnp.inf)
        sc = jnp.where(sc == 0.0, 0.0, sc)
        bits = pltpu.bitcast(sc, jnp.int32)
        key_scr[c] = bits ^ ((bits >> 31) & 0x7FFFFFFF)

    def plain_body(c, carry):
        score_chunk(c, False)
        return carry

    lax.fori_loop(0, i, plain_body, 0)
    score_chunk(i, True)

    kf = float(topk)

    def count(pred):
        def body(c, part):
            ones = jnp.where(pred(key_scr[c]), 1.0, 0.0)
            return part + jnp.sum(ones.reshape(tq // 8, 8, tq), axis=0)
        part = lax.fori_loop(0, nch, body, jnp.zeros((8, tq), F32))
        return jnp.sum(part, axis=0, keepdims=True)

    def refine(cand, thr):
        return jnp.where(count(lambda k: k >= cand) >= kf, cand, thr)

    thr = jnp.full((1, tq), INT_MIN, jnp.int32)
    thr = refine(jnp.zeros((1, tq), jnp.int32), thr)
    thr = lax.fori_loop(
        0, 31, lambda it, t: refine(t | jnp.left_shift(jnp.int32(1), 30 - it), t), thr)

    need = kf - count(lambda k: k > thr)
    need = jnp.where(thr == KEY_NEG_INF, 0.0, need)
    earlier = jnp.where(col < row, 1.0, 0.0).astype(BF16)
    ones_rows = jnp.ones((8, tq), BF16)

    def sel_body(c, off):
        k = key_scr[c]
        eq = k == thr
        eqb = jnp.where(eq, 1.0, 0.0).astype(BF16)
        before = jnp.dot(earlier, eqb, preferred_element_type=F32) + off
        tie = jnp.where(before < need, 0.0, NEG)
        bias_t = jnp.where(k > thr, 0.0, jnp.where(eq, tie, NEG))
        bias_scr[c] = jnp.transpose(bias_t)
        return off + jnp.dot(ones_rows, eqb, preferred_element_type=F32)[0:1, :]

    lax.fori_loop(0, nch, sel_body, jnp.zeros((1, tq), F32))

    for g in range(DSA_W // LANES):
        cols = slice(g * LANES, (g + 1) * LANES)

        def load_k(c, cols=cols):
            return k_ref[pl.ds(pl.multiple_of(c * tq, tq), tq), cols]

        def load_v(c, cols=cols):
            return v_ref[pl.ds(pl.multiple_of(c * tq, tq), tq), cols]

        def bias_fn(a, c, s, diag):
            return s + bias_scr[c]

        out = _flash_pair(q_ref[:, cols], load_k, load_v, bias_fn, nch, None, s_scr, tq)
        o_ref[:, cols] = out.astype(o_ref.dtype)


def _dsa_attention(main, small, batch, seq):
    tq = min(ATT_BLOCK, seq)
    nq = seq // tq
    n = batch * seq
    topk = min(DSA_TOPK_MAX, seq // 4)
    assert topk <= tq
    iq_block = MAIN_COLS // 256 - 1
    return pl.pallas_call(
        functools.partial(_dsa_kernel, tq=tq, topk=topk),
        grid=(batch, nq),
        in_specs=[pl.BlockSpec((tq, DSA_W), lambda b, i: (b * nq + i, 3)),
                  pl.BlockSpec((seq, DSA_W), lambda b, i: (b, 4)),
                  pl.BlockSpec((seq, DSA_W), lambda b, i: (b, 5)),
                  pl.BlockSpec((tq, 256), lambda b, i: (b * nq + i, iq_block)),
                  pl.BlockSpec((seq, LANES), lambda b, i: (b, 0)),
                  pl.BlockSpec((tq, LANES), lambda b, i: (b * nq + i, 1))],
        out_specs=pl.BlockSpec((tq, DSA_W), lambda b, i: (b * nq + i, 0)),
        out_shape=jax.ShapeDtypeStruct((n, DSA_W), BF16),
        scratch_shapes=[pltpu.VMEM((nq, tq, tq), jnp.int32),
                        pltpu.VMEM((nq, tq, tq), F32),
                        pltpu.VMEM((seq, LANES), BF16),
                        pltpu.VMEM((2, nq, tq, tq), F32)],
        compiler_params=_cparams(("parallel", "arbitrary")),
    )(main, main, main, main, small, small)


def _matmul_ln_kernel(*refs, n_pairs):
    a_refs = refs[:n_pairs]
    w_refs = refs[n_pairs:2 * n_pairs]
    bias_ref, x_ref, g_ref, b_ref, o_ref, ob_ref = refs[2 * n_pairs:]
    acc = bias_ref[...] + ALPHA * x_ref[...]
    for a_ref, w_ref in zip(a_refs, w_refs):
        acc = acc + jnp.dot(a_ref[...], w_ref[...], preferred_element_type=F32)
    y = _layer_norm(acc, g_ref[...], b_ref[...])
    o_ref[...] = y
    ob_ref[...] = y.astype(BF16)


def _matmul_ln(a_list, w_list, bias, x, g, b, tm):
    n, d = x.shape
    n_pairs = len(a_list)
    row = lambda i: (i, 0)
    fixed = lambda i: (0, 0)
    in_specs = ([pl.BlockSpec((tm, a.shape[1]), row) for a in a_list]
                + [pl.BlockSpec(w.shape, fixed) for w in w_list]
                + [pl.BlockSpec((1, d), fixed), pl.BlockSpec((tm, d), row),
                   pl.BlockSpec((1, d), fixed), pl.BlockSpec((1, d), fixed)])
    return pl.pallas_call(
        functools.partial(_matmul_ln_kernel, n_pairs=n_pairs),
        grid=(n // tm,),
        in_specs=in_specs,
        out_specs=[pl.BlockSpec((tm, d), row), pl.BlockSpec((tm, d), row)],
        out_shape=[jax.ShapeDtypeStruct((n, d), F32), jax.ShapeDtypeStruct((n, d), BF16)],
        compiler_params=_cparams(("parallel",)),
    )(*a_list, *w_list, bias, x, g, b)


def _swiglu_kernel(xb_ref, wg_ref, wu_ref, wd_ref, x_ref, g_ref, b_ref, o_ref, ob_ref, acc_ref):
    f = pl.program_id(1)

    @pl.when(f == 0)
    def _():
        acc_ref[...] = ALPHA * x_ref[...]

    xb = xb_ref[...]
    hg = jnp.dot(xb, wg_ref[...], preferred_element_type=F32)
    hu = jnp.dot(xb, wu_ref[...], preferred_element_type=F32)
    h = hg * _sigmoid(hg) * hu
    acc_ref[...] += jnp.dot(h.astype(BF16), wd_ref[...], preferred_element_type=F32)

    @pl.when(f == pl.num_programs(1) - 1)
    def _():
        y = _layer_norm(acc_ref[...], g_ref[...], b_ref[...])
        o_ref[...] = y
        ob_ref[...] = y.astype(BF16)


def _swiglu_ln(xb, x, wg, wu, wd, g, b, tm, tf):
    n, d = x.shape
    ffn = wg.shape[1]
    row = lambda i, f: (i, 0)
    fixed = lambda i, f: (0, 0)
    return pl.pallas_call(
        _swiglu_kernel,
        grid=(n // tm, ffn // tf),
        in_specs=[pl.BlockSpec((tm, d), row),
                  pl.BlockSpec((d, tf), lambda i, f: (0, f)),
                  pl.BlockSpec((d, tf), lambda i, f: (0, f)),
                  pl.BlockSpec((tf, d), lambda i, f: (f, 0)),
                  pl.BlockSpec((tm, d), row),
                  pl.BlockSpec((1, d), fixed), pl.BlockSpec((1, d), fixed)],
        out_specs=[pl.BlockSpec((tm, d), row), pl.BlockSpec((tm, d), row)],
        out_shape=[jax.ShapeDtypeStruct((n, d), F32), jax.ShapeDtypeStruct((n, d), BF16)],
        scratch_shapes=[pltpu.VMEM((tm, d), F32)],
        compiler_params=_cparams(("parallel", "arbitrary")),
    )(xb, wg, wu, wd, x, g, b)


def _glu_kernel(x_ref, wa_ref, wg_ref, ba_ref, bg_ref, o_ref):
    xb = x_ref[...]
    a = jnp.dot(xb, wa_ref[...], preferred_element_type=F32) + ba_ref[...]
    gate = jnp.dot(xb, wg_ref[...], preferred_element_type=F32) + bg_ref[...]
    o_ref[...] = a * _sigmoid(gate)


def _glu_proj(xb, w, bias, tm, tn):
    n, d = xb.shape
    c = w.shape[1] // 2
    nc = c // tn
    return pl.pallas_call(
        _glu_kernel,
        grid=(n // tm, nc),
        in_specs=[pl.BlockSpec((tm, d), lambda i, j: (i, 0)),
                  pl.BlockSpec((d, tn), lambda i, j: (0, j)),
                  pl.BlockSpec((d, tn), lambda i, j: (0, nc + j)),
                  pl.BlockSpec((1, tn), lambda i, j: (0, j)),
                  pl.BlockSpec((1, tn), lambda i, j: (0, nc + j))],
        out_specs=pl.BlockSpec((tm, tn), lambda i, j: (i, j)),
        out_shape=jax.ShapeDtypeStruct((n, c), F32),
        compiler_params=_cparams(("parallel", "arbitrary")),
    )(xb, w, w, bias, bias)


def _dwconv_kernel(cur_ref, prev_ref, w_ref, bdw_ref, g_ref, b_ref, o_ref, buf_ref, sh_ref, y_ref,
                   *, ts, halo):
    i = pl.program_id(1)
    prev = prev_ref[...]
    buf_ref[0:halo, :] = jnp.where(i == 0, jnp.zeros_like(prev), prev)
    buf_ref[halo:halo + ts, :] = cur_ref[...]
    base = halo - (CONV_WIDTH - 1)
    span = sh_ref.shape[1]
    for cb in range(cur_ref.shape[1] // LANES):
        cols = slice(cb * LANES, (cb + 1) * LANES)
        for r in range(1, 8):
            sh_ref[r - 1] = buf_ref[r:r + span, cols]
        acc = jnp.zeros((ts, LANES), F32)
        for j in range(CONV_WIDTH):
            a, r = divmod(base + j, 8)
            if r == 0:
                win = buf_ref[8 * a:8 * a + ts, cols]
            else:
                win = sh_ref[r - 1, 8 * a:8 * a + ts, :]
            acc = acc + w_ref[j:j + 1, cols] * win
        y_ref[:, cols] = acc
    y = _layer_norm(y_ref[...] + bdw_ref[...], g_ref[...], b_ref[...])
    o_ref[...] = (y * _sigmoid(y)).astype(o_ref.dtype)


def _dwconv_ln_silu(h, w_dw, b_dw, g, b, batch, seq, ts):
    n, c = h.shape
    halo = 32
    nt = seq // ts
    ratio = ts // halo
    fixed = lambda bi, i: (0, 0)
    cur = lambda bi, i: (bi * nt + i, 0)
    prev = lambda bi, i: (jnp.maximum((bi * nt + i) * ratio - 1, 0), 0)
    return pl.pallas_call(
        functools.partial(_dwconv_kernel, ts=ts, halo=halo),
        grid=(batch, nt),
        in_specs=[pl.BlockSpec((ts, c), cur), pl.BlockSpec((halo, c), prev),
                  pl.BlockSpec((CONV_WIDTH, c), fixed),
                  pl.BlockSpec((1, c), fixed), pl.BlockSpec((1, c), fixed),
                  pl.BlockSpec((1, c), fixed)],
        out_specs=pl.BlockSpec((ts, c), cur),
        out_shape=jax.ShapeDtypeStruct((n, c), BF16),
        scratch_shapes=[pltpu.VMEM((halo + ts, c), F32),
                        pltpu.VMEM((7, ts + halo - 8, LANES), F32),
                        pltpu.VMEM((ts, c), F32)],
        compiler_params=_cparams(("parallel", "arbitrary")),
    )(h, h, w_dw, b_dw, g, b)


MOE_TILE = 512
MOE_EXPERT_TILE = 1024
NOT_ROUTED = -1e9


def _router_kernel(x_ref, w_ref, comb_ref, rank_ref, cum_ref, count_scr):
    @pl.when(pl.program_id(0) == 0)
    def _():
        count_scr[...] = jnp.zeros_like(count_scr)

    logits = jnp.dot(x_ref[...], w_ref[...], precision=lax.Precision.HIGHEST,
                     preferred_element_type=F32)
    tm = logits.shape[0]
    lane = lax.broadcasted_iota(jnp.int32, logits.shape, 1)
    logits = jnp.where(lane < N_EXPERTS, logits, -jnp.inf)
    m1 = jnp.max(logits, axis=1, keepdims=True)
    i1 = jnp.min(jnp.where(logits == m1, lane, LANES), axis=1, keepdims=True)
    rest = jnp.where(lane == i1, -jnp.inf, logits)
    m2 = jnp.max(rest, axis=1, keepdims=True)
    i2 = jnp.min(jnp.where(rest == m2, lane, LANES), axis=1, keepdims=True)
    e2 = jnp.exp(m2 - m1)
    g1 = 1.0 / (1.0 + e2)
    g2 = e2 / (1.0 + e2)
    comb_ref[...] = jnp.where(lane == i1, g1, 0.0) + jnp.where(lane == i2, g2, 0.0)

    routed = jnp.where(lane == i1, 1.0, jnp.where(lane == i2, 1.0, 0.0))
    r = lax.broadcasted_iota(jnp.int32, (tm, tm), 0)
    c = lax.broadcasted_iota(jnp.int32, (tm, tm), 1)
    earlier = jnp.where(c < r, 1.0, 0.0).astype(BF16)
    before = jnp.dot(earlier, routed.astype(BF16), preferred_element_type=F32)
    start = count_scr[0:1, :]
    rank_ref[...] = jnp.where(routed > 0.0, before + start, NOT_ROUTED)
    after = start + before[tm - 1:tm, :] + routed[tm - 1:tm, :]
    count_scr[...] = jnp.broadcast_to(after, count_scr.shape)
    cum_ref[...] = jnp.broadcast_to(after, cum_ref.shape)


def _router(x, w_router_padded, tm):
    n, d = x.shape
    row = lambda i: (i, 0)
    return pl.pallas_call(
        _router_kernel,
        grid=(n // tm,),
        in_specs=[pl.BlockSpec((tm, d), row),
                  pl.BlockSpec((d, LANES), lambda i: (0, 0))],
        out_specs=[pl.BlockSpec((tm, LANES), row), pl.BlockSpec((tm, LANES), row),
                   pl.BlockSpec((8, LANES), row)],
        out_shape=[jax.ShapeDtypeStruct((n, LANES), F32), jax.ShapeDtypeStruct((n, LANES), F32),
                   jax.ShapeDtypeStruct((n // tm * 8, LANES), F32)],
        scratch_shapes=[pltpu.VMEM((8, LANES), F32)],
        compiler_params=_cparams(("arbitrary",)),
    )(x, w_router_padded)


def _dispatch_kernel(g_idx, t_idx, valid, e_of_g, slot_ref, x_ref, comb_ref, xg_ref, gate_ref,
                     acc_ref, gacc_ref):
    i = pl.program_id(0)
    g = g_idx[i]
    first = jnp.logical_or(i == 0, g_idx[jnp.maximum(i - 1, 0)] != g)
    last = jnp.logical_or(i == pl.num_programs(0) - 1,
                          g_idx[jnp.minimum(i + 1, pl.num_programs(0) - 1)] != g)

    @pl.when(first)
    def _():
        acc_ref[...] = jnp.zeros_like(acc_ref)
        gacc_ref[...] = jnp.zeros_like(gacc_ref)

    @pl.when(valid[i] > 0)
    def _():
        t = acc_ref.shape[0]
        e = e_of_g[g]
        sub = lax.broadcasted_iota(jnp.int32, (8, t), 0)
        slot_row = jnp.sum(jnp.where(sub == e, slot_ref[...], 0.0), axis=0, keepdims=True)
        want = (g * t + lax.broadcasted_iota(jnp.int32, (t, t), 0)).astype(F32)
        sel = jnp.where(slot_row == want, 1.0, 0.0).astype(BF16)
        acc_ref[...] += jnp.dot(sel, x_ref[...], preferred_element_type=F32)
        comb = comb_ref[...]
        hi = comb.astype(BF16)
        lo = (comb - hi.astype(F32)).astype(BF16)
        gacc_ref[...] += (jnp.dot(sel, hi, preferred_element_type=F32)
                          + jnp.dot(sel, lo, preferred_element_type=F32))

    @pl.when(last)
    def _():
        xg_ref[...] = acc_ref[...].astype(xg_ref.dtype)
        gate_ref[...] = gacc_ref[...]


def _dispatch(work, e_of_g, slot_rows, xb, comb, n_slots):
    g_idx, t_idx, valid = work
    n, d = xb.shape
    t = MOE_TILE
    tok = lambda i, g, ti, v, e: (ti[i], 0)
    slot = lambda i, g, ti, v, e: (g[i], 0)
    return pl.pallas_call(
        _dispatch_kernel,
        grid_spec=pltpu.PrefetchScalarGridSpec(
            num_scalar_prefetch=4,
            grid=(g_idx.shape[0],),
            in_specs=[pl.BlockSpec((8, t), lambda i, g, ti, v, e: (0, ti[i])),
                      pl.BlockSpec((t, d), tok), pl.BlockSpec((t, LANES), tok)],
            out_specs=[pl.BlockSpec((t, d), slot), pl.BlockSpec((t, LANES), slot)],
            scratch_shapes=[pltpu.VMEM((t, d), F32), pltpu.VMEM((t, LANES), F32)]),
        out_shape=[jax.ShapeDtypeStruct((n_slots, d), BF16),
                   jax.ShapeDtypeStruct((n_slots, LANES), F32)],
        compiler_params=_cparams(("arbitrary",)),
    )(g_idx, t_idx, valid, e_of_g, slot_rows, xb, comb)


def _expert_kernel(e_of_big, nvalid, xg_ref, gate_ref, wg_ref, wu_ref, wd_ref, y_ref, acc_ref):
    b = pl.program_id(0)
    f = pl.program_id(1)
    nv = nvalid[b]

    @pl.when(nv > 0)
    def _():
        @pl.when(f == 0)
        def _():
            acc_ref[...] = jnp.zeros_like(acc_ref)

        tm = acc_ref.shape[0]
        e = e_of_big[b]
        live = lax.broadcasted_iota(jnp.int32, (tm, 1), 0) < nv
        xg = xg_ref[...]
        xg = jnp.where(live, xg, jnp.zeros_like(xg))
        gates = gate_ref[...]
        lane = lax.broadcasted_iota(jnp.int32, gates.shape, 1)
        ge = jnp.sum(jnp.where(jnp.logical_and(lane == e, live), gates, 0.0),
                     axis=1, keepdims=True)
        hg = jnp.dot(xg, wg_ref[...], preferred_element_type=F32)
        hu = jnp.dot(xg, wu_ref[...], preferred_element_type=F32)
        h = hg * _sigmoid(hg) * hu * ge
        acc_ref[...] += jnp.dot(h.astype(BF16), wd_ref[...], preferred_element_type=F32)

        @pl.when(f == pl.num_programs(1) - 1)
        def _():
            y_ref[...] = acc_ref[...].astype(y_ref.dtype)


def _expert_ffn(e_of_big, nvalid, xg, gate, wg, wu, wd, tf):
    n_slots, d = xg.shape
    edim = wg.shape[2]
    tm = MOE_EXPERT_TILE
    row = lambda b, f, e, nv: (b, 0)
    return pl.pallas_call(
        _expert_kernel,
        grid_spec=pltpu.PrefetchScalarGridSpec(
            num_scalar_prefetch=2,
            grid=(n_slots // tm, edim // tf),
            in_specs=[pl.BlockSpec((tm, d), row), pl.BlockSpec((tm, LANES), row),
                      pl.BlockSpec((None, d, tf), lambda b, f, e, nv: (e[b], 0, f)),
                      pl.BlockSpec((None, d, tf), lambda b, f, e, nv: (e[b], 0, f)),
                      pl.BlockSpec((None, tf, d), lambda b, f, e, nv: (e[b], f, 0))],
            out_specs=pl.BlockSpec((tm, d), row),
            scratch_shapes=[pltpu.VMEM((tm, d), F32)]),
        out_shape=jax.ShapeDtypeStruct((n_slots, d), BF16),
        compiler_params=_cparams(("parallel", "arbitrary")),
    )(e_of_big, nvalid, xg, gate, wg, wu, wd)


def _combine_kernel(t_idx, g_idx, valid, e_of_g, slot_ref, y_ref, x_ref, g_ref, b_ref,
                    o_ref, ob_ref, acc_ref):
    i = pl.program_id(0)
    ti = t_idx[i]
    first = jnp.logical_or(i == 0, t_idx[jnp.maximum(i - 1, 0)] != ti)
    last = jnp.logical_or(i == pl.num_programs(0) - 1,
                          t_idx[jnp.minimum(i + 1, pl.num_programs(0) - 1)] != ti)

    @pl.when(first)
    def _():
        acc_ref[...] = ALPHA * x_ref[...]

    @pl.when(valid[i] > 0)
    def _():
        t = acc_ref.shape[0]
        g = g_idx[i]
        e = e_of_g[g]
        slots = slot_ref[...]
        lane = lax.broadcasted_iota(jnp.int32, slots.shape, 1)
        slot_col = jnp.sum(jnp.where(lane == e, slots, 0.0), axis=1, keepdims=True)
        want = (g * t + lax.broadcasted_iota(jnp.int32, (t, t), 1)).astype(F32)
        sel = jnp.where(slot_col == want, 1.0, 0.0).astype(BF16)
        acc_ref[...] += jnp.dot(sel, y_ref[...], preferred_element_type=F32)

    @pl.when(last)
    def _():
        y = _layer_norm(acc_ref[...], g_ref[...], b_ref[...])
        o_ref[...] = y
        ob_ref[...] = y.astype(BF16)


def _combine_ln(work, e_of_g, slot_cols, y, x, g, b):
    t_idx, g_idx, valid = work
    n, d = x.shape
    t = MOE_TILE
    tok = lambda i, ti, gi, v, e: (ti[i], 0)
    fixed = lambda i, ti, gi, v, e: (0, 0)
    return pl.pallas_call(
        _combine_kernel,
        grid_spec=pltpu.PrefetchScalarGridSpec(
            num_scalar_prefetch=4,
            grid=(t_idx.shape[0],),
            in_specs=[pl.BlockSpec((t, LANES), tok),
                      pl.BlockSpec((t, d), lambda i, ti, gi, v, e: (gi[i], 0)),
                      pl.BlockSpec((t, d), tok),
                      pl.BlockSpec((1, d), fixed), pl.BlockSpec((1, d), fixed)],
            out_specs=[pl.BlockSpec((t, d), tok), pl.BlockSpec((t, d), tok)],
            scratch_shapes=[pltpu.VMEM((t, d), F32)]),
        out_shape=[jax.ShapeDtypeStruct((n, d), F32), jax.ShapeDtypeStruct((n, d), BF16)],
        compiler_params=_cparams(("arbitrary",)),
    )(t_idx, g_idx, valid, e_of_g, slot_cols, y, x, g, b)


def _routing_tables(rank, cum, n):
    t, te = MOE_TILE, MOE_EXPERT_TILE
    nt = n // t
    i32 = jnp.int32
    after = cum.reshape(nt, 8, LANES)[:, 0, :N_EXPERTS].astype(i32)
    before = jnp.concatenate([jnp.zeros((1, N_EXPERTS), i32), after[:-1]], axis=0)
    totals = after[-1]
    padded = (totals + te - 1) // te * te
    off = jnp.cumsum(padded) - padded
    n_slots = 2 * n + N_EXPERTS * te
    n_big = n_slots // te
    n_g = n_slots // t

    big_start = jnp.arange(n_big, dtype=i32) * te
    e_of_big = jnp.minimum(jnp.sum(big_start[:, None] >= (off + padded)[None, :], axis=1),
                           N_EXPERTS - 1).astype(i32)
    nvalid = jnp.clip(totals[e_of_big] - (big_start - off[e_of_big]), 0, te).astype(i32)
    e_of_g = jnp.repeat(e_of_big, te // t)

    off_row = jnp.concatenate([off.astype(F32), jnp.zeros((LANES - N_EXPERTS,), F32)])[None, :]
    slot_cols = rank + off_row
    slot_rows = jnp.transpose(slot_cols[:, :N_EXPERTS])

    has = after > before
    g0 = (off[None, :] + before) // t
    g1 = (off[None, :] + jnp.maximum(after, 1) - 1) // t
    t_ids = jnp.broadcast_to(jnp.arange(nt, dtype=i32)[:, None], (nt, N_EXPERTS))
    cand_g = jnp.stack([g0, g0 + 1], axis=-1).reshape(-1).astype(i32)
    cand_t = jnp.stack([t_ids, t_ids], axis=-1).reshape(-1)
    cand_ok = jnp.stack([has, jnp.logical_and(has, g1 > g0)], axis=-1).reshape(-1)
    n_items = min(N_EXPERTS * nt + n_g, 2 * N_EXPERTS * nt)
    count = jnp.sum(cand_ok.astype(i32))
    big = jnp.int32(2 ** 30)

    def ordered(major, minor, minor_range):
        order = jnp.argsort(jnp.where(cand_ok, major * minor_range + minor, big))[:n_items]
        pos = jnp.minimum(jnp.arange(n_items, dtype=i32), count - 1)
        pick = order[pos]
        ok = (jnp.arange(n_items, dtype=i32) < count).astype(i32)
        return cand_g[pick], cand_t[pick], ok

    dg, dt, dok = ordered(cand_g, cand_t, nt)
    cg, ct, cok = ordered(cand_t, cand_g, n_g)
    return dict(n_slots=n_slots, e_of_big=e_of_big, nvalid=nvalid, e_of_g=e_of_g,
                slot_cols=slot_cols, slot_rows=slot_rows,
                dispatch=(dg, dt, dok), combine=(ct, cg, cok))


def _moe_ln(xb, x, w_router_padded, wg, wu, wd, g, b, tf):
    n, _ = x.shape
    comb, rank, cum = _router(x, w_router_padded, MOE_TILE)
    rt = _routing_tables(rank, cum, n)
    xg, gate = _dispatch(rt["dispatch"], rt["e_of_g"], rt["slot_rows"], xb, comb, rt["n_slots"])
    y = _expert_ffn(rt["e_of_big"], rt["nvalid"], xg, gate, wg, wu, wd, tf)
    return _combine_ln(rt["combine"], rt["e_of_g"], rt["slot_cols"], y, x, g, b)


def _rope_patterns(seq):
    half = ROPE_DIM // 2
    inv_freq = ROPE_THETA ** (-jnp.arange(0, ROPE_DIM, 2, dtype=F32) / ROPE_DIM)
    ang = jnp.arange(seq, dtype=F32)[:, None] * inv_freq[None, :]
    cos, sin = jnp.cos(ang), jnp.sin(ang)
    rest = HEAD_DIM - ROPE_DIM
    c = jnp.concatenate([cos, cos, jnp.ones((seq, rest), F32)], axis=1)
    s1 = jnp.concatenate([-sin, jnp.zeros((seq, half + rest), F32)], axis=1)
    s2 = jnp.concatenate([jnp.zeros((seq, half), F32), sin, jnp.zeros((seq, rest), F32)], axis=1)
    return c, s1, s2


def _row_tile(n, want):
    return want if n % want == 0 else n


def _attention_layer(x, xb, w_in, b_f, w_out, g, b, tabs, batch, seq):
    n = batch * seq
    o = [0, 512, 1024, 1536, 1544, 2056, 2568, 3080, 3336, 3400, 3404]
    part = lambda k: w_in[:, o[k]:o[k + 1]]
    scale = HEAD_DIM ** -0.5
    pad = jnp.zeros((D_MODEL, LANES - N_FOX_HEADS - N_IDX_HEADS), F32)
    w_all = jnp.concatenate([part(0) * scale, part(1), part(2), part(4) * scale, part(5),
                             part(6), part(7), part(8), part(8), part(3), part(9), pad],
                            axis=1).astype(BF16)
    main, small = _project(xb, w_all, tabs, seq, _row_tile(seq, 512))
    bias_row = jnp.concatenate([b_f.astype(F32), jnp.zeros((LANES - N_FOX_HEADS,), F32)])[None, :]
    fcum = _forget_cumsum(small, bias_row, batch, seq)
    tq = min(ATT_BLOCK, seq)
    frow = jnp.transpose(fcum.reshape(batch, seq, LANES)[:, :, :N_FOX_HEADS], (0, 2, 1))
    frow = frow.reshape(batch, N_FOX_HEADS, seq // tq, tq)
    fox = _fox_attention(main, fcum, frow, batch, seq)
    dsa = _dsa_attention(main, small, batch, seq)
    wo = w_out.astype(BF16)
    zero_bias = jnp.zeros((1, D_MODEL), F32)
    return _matmul_ln([fox, dsa], [wo[:FOX_W], wo[FOX_W:]], zero_bias, x, g, b,
                      _row_tile(n, 512))


def kernel(x, norm_mix_g, norm_mix_b, norm_ffn_g, norm_ffn_b, attn_w_in, attn_b_f, attn_w_out,
           ffn_w_gate, ffn_w_up, ffn_w_down, conv_w_in, conv_b_in, conv_w_dw, conv_b_dw,
           conv_ln_g, conv_ln_b, conv_w_out, conv_b_out, moe_w_router, moe_w_gate, moe_w_up,
           moe_w_down):
    batch, seq, d = x.shape
    n = batch * seq
    tabs = tuple(jnp.tile(p, (1, MAIN_TN // HEAD_DIM)) for p in _rope_patterns(seq))
    r2 = lambda v: v.astype(F32).reshape(1, -1)

    xf = x.reshape(n, d)
    xb = xf.astype(BF16)
    for layer in range(DEPTH):
        j = layer // 2
        gm, bm = r2(norm_mix_g[layer]), r2(norm_mix_b[layer])
        gf, bf = r2(norm_ffn_g[layer]), r2(norm_ffn_b[layer])
        if layer % 2 == 0:
            xf, xb = _attention_layer(xf, xb, attn_w_in[j], attn_b_f[j], attn_w_out[j], gm, bm,
                                      tabs, batch, seq)
            xf, xb = _swiglu_ln(xb, xf, ffn_w_gate[j].astype(BF16), ffn_w_up[j].astype(BF16),
                                ffn_w_down[j].astype(BF16), gf, bf, _row_tile(n, 1024), 256)
        else:
            h = _glu_proj(xb, conv_w_in[j].astype(BF16), r2(conv_b_in[j]), _row_tile(n, 1024), 512)
            hb = _dwconv_ln_silu(h, conv_w_dw[j].astype(F32), r2(conv_b_dw[j]),
                                 r2(conv_ln_g[j]), r2(conv_ln_b[j]), batch, seq,
                                 min(256, seq))
            xf, xb = _matmul_ln([hb], [conv_w_out[j].astype(BF16)], r2(conv_b_out[j]), xf, gm, bm,
                                _row_tile(n, 512))
            w_r = jnp.concatenate([moe_w_router[j].astype(F32),
                                   jnp.zeros((d, LANES - N_EXPERTS), F32)], axis=1)
            xf, xb = _moe_ln(xb, xf, w_r, moe_w_gate[j].astype(BF16), moe_w_up[j].astype(BF16),
                             moe_w_down[j].astype(BF16), gf, bf, 512)
    return xf.reshape(batch, seq, d)
```

```python
import functools

import jax
import jax.numpy as jnp
from jax import lax
from jax.experimental import pallas as pl
from jax.experimental.pallas import tpu as pltpu

F32 = jnp.float32
BF16 = jnp.bfloat16

D_MODEL = 1024
DEPTH = 4
HEAD_DIM = 64
FOX_W = 512
DSA_W = 512
N_FOX_HEADS = 8
N_IDX_HEADS = 4
IDX_DIM = 64
DSA_TOPK_MAX = 256
ROPE_DIM = 16
ROPE_THETA = 500000.0
CONV_WIDTH = 31
N_EXPERTS = 8
LN_EPS = 1e-5
ALPHA = (2.0 * DEPTH) ** 0.25

LANES = 128
NEG = -1e30
INT_MIN = -2 ** 31
KEY_NEG_INF = -2 ** 31 + 0x007FFFFF
VMEM_LIMIT = 56 * 1024 * 1024

ATT_BLOCK = 512
MAIN_COLS = 3328
MAIN_TN = 256
ROPE_TILES = (6, 7, 8, 9, 12)


def _cparams(sem):
    return pltpu.CompilerParams(dimension_semantics=sem, vmem_limit_bytes=VMEM_LIMIT)


def _layer_norm(y, g, b):
    mu = jnp.mean(y, axis=-1, keepdims=True)
    d = y - mu
    var = jnp.mean(d * d, axis=-1, keepdims=True)
    return d * lax.rsqrt(var + LN_EPS) * g + b


def _sigmoid(z):
    return 1.0 / (1.0 + jnp.exp(-z))


def _rope(acc, c, s1, s2):
    tn = acc.shape[1]
    return acc * c + pltpu.roll(acc, tn - 8, 1) * s1 + pltpu.roll(acc, 8, 1) * s2


def _proj_kernel(x_ref, w_ref, c_ref, s1_ref, s2_ref, main_ref, small_ref):
    x = x_ref[...]
    for j in range(MAIN_COLS // MAIN_TN):
        cols = slice(j * MAIN_TN, (j + 1) * MAIN_TN)
        acc = jnp.dot(x, w_ref[:, cols], preferred_element_type=F32)
        if j in ROPE_TILES:
            acc = _rope(acc, c_ref[...], s1_ref[...], s2_ref[...])
        main_ref[:, cols] = acc.astype(main_ref.dtype)
    acc = jnp.dot(x, w_ref[:, MAIN_COLS:MAIN_COLS + 2 * LANES], preferred_element_type=F32)
    small_ref[:, :LANES] = _rope(acc[:, :LANES], c_ref[:, :LANES], s1_ref[:, :LANES],
                                 s2_ref[:, :LANES])
    small_ref[:, LANES:] = acc[:, LANES:]


def _project(xb, w, tables, seq, tm):
    n, d = xb.shape
    pos_blocks = seq // tm
    tab_spec = pl.BlockSpec((tm, MAIN_TN), lambda i: (i % pos_blocks, 0))
    row = lambda i: (i, 0)
    return pl.pallas_call(
        _proj_kernel,
        grid=(n // tm,),
        in_specs=[pl.BlockSpec((tm, d), row),
                  pl.BlockSpec(w.shape, lambda i: (0, 0)),
                  tab_spec, tab_spec, tab_spec],
        out_specs=[pl.BlockSpec((tm, MAIN_COLS), row), pl.BlockSpec((tm, 2 * LANES), row)],
        out_shape=[jax.ShapeDtypeStruct((n, MAIN_COLS), BF16),
                   jax.ShapeDtypeStruct((n, 2 * LANES), F32)],
        compiler_params=_cparams(("parallel",)),
    )(xb, w, *tables)


def _forget_kernel(z_ref, b_ref, o_ref, *, chunk):
    s = z_ref.shape[0]
    z = z_ref[...] + b_ref[...]
    logf = jnp.minimum(z, 0.0) - jnp.log(1.0 + jnp.exp(-jnp.abs(z)))
    r = lax.broadcasted_iota(jnp.int32, (chunk, chunk), 0)
    c = lax.broadcasted_iota(jnp.int32, (chunk, chunk), 1)
    tri = jnp.where(c <= r, 1.0, 0.0).astype(F32)
    carry = jnp.zeros((1, LANES), F32)
    for k in range(s // chunk):
        blk = logf[k * chunk:(k + 1) * chunk, :]
        cs = jnp.dot(tri, blk, precision=lax.Precision.HIGHEST,
                     preferred_element_type=F32) + carry
        o_ref[k * chunk:(k + 1) * chunk, :] = cs
        carry = cs[chunk - 1:chunk, :]


def _forget_cumsum(small, bias_row, batch, seq):
    chunk = min(256, seq)
    return pl.pallas_call(
        functools.partial(_forget_kernel, chunk=chunk),
        grid=(batch,),
        in_specs=[pl.BlockSpec((seq, LANES), lambda b: (b, 1)),
                  pl.BlockSpec((1, LANES), lambda b: (0, 0))],
        out_specs=pl.BlockSpec((seq, LANES), lambda b: (b, 0)),
        out_shape=jax.ShapeDtypeStruct((batch * seq, LANES), F32),
        compiler_params=_cparams(("parallel",)),
    )(small, bias_row)


def _flash_pair(q2, load_k, load_v, bias_fn, n_plain, diag_chunk, s_scr, tq):
    lane = lax.broadcasted_iota(jnp.int32, (tq, LANES), 1)
    first = lane < HEAD_DIM
    qa = (jnp.where(first, q2, jnp.zeros_like(q2)), jnp.where(first, jnp.zeros_like(q2), q2))
    folds = s_scr.shape[3] // LANES

    def score(c, mx, diag):
        kc = load_k(c)
        new = []
        for a in range(2):
            s = lax.dot_general(qa[a], kc, (((1,), (1,)), ((), ())),
                                preferred_element_type=F32)
            s = bias_fn(a, c, s, diag)
            s_scr[a, c] = s
            m = mx[a]
            for f in range(folds):
                m = jnp.maximum(m, s[:, f * LANES:(f + 1) * LANES])
            new.append(m)
        return tuple(new)

    mx = tuple(jnp.full((tq, LANES), NEG, F32) for _ in range(2))
    mx = lax.fori_loop(0, n_plain, lambda c, m: score(c, m, False), mx)
    n_all = n_plain
    if diag_chunk is not None:
        mx = score(diag_chunk, mx, True)
        n_all = n_plain + 1
    row_max = [jnp.max(m, axis=1, keepdims=True) for m in mx]

    def weigh(c, carry):
        vc = load_v(c)
        new = []
        for a in range(2):
            l, acc = carry[a]
            p = jnp.exp(s_scr[a, c] - row_max[a])
            for f in range(folds):
                l = l + p[:, f * LANES:(f + 1) * LANES]
            acc = acc + jnp.dot(p.astype(BF16), vc, preferred_element_type=F32)
            new.append((l, acc))
        return tuple(new)

    init = tuple((jnp.zeros((tq, LANES), F32), jnp.zeros((tq, LANES), F32)) for _ in range(2))
    carry = lax.fori_loop(0, n_all, weigh, init)
    o0 = carry[0][1] / jnp.sum(carry[0][0], axis=1, keepdims=True)
    o1 = carry[1][1] / jnp.sum(carry[1][0], axis=1, keepdims=True)
    return jnp.where(first, o0, o1)


def _fox_kernel(q_ref, k_ref, v_ref, fc_ref, fr_ref, o_ref, s_scr, *, tq):
    hp = pl.program_id(1)
    i = pl.program_id(2)
    lane = lax.broadcasted_iota(jnp.int32, (tq, LANES), 1)
    fc = fc_ref[...]
    fq = [jnp.sum(jnp.where(lane == hp * 2 + a, fc, 0.0), axis=1, keepdims=True)
          for a in range(2)]
    row = lax.broadcasted_iota(jnp.int32, (tq, tq), 0)
    col = lax.broadcasted_iota(jnp.int32, (tq, tq), 1)

    def load_k(c):
        return k_ref[pl.ds(pl.multiple_of(c * tq, tq), tq), :]

    def load_v(c):
        return v_ref[pl.ds(pl.multiple_of(c * tq, tq), tq), :]

    def bias_fn(a, c, s, diag):
        fk = fr_ref[0, pl.ds(hp * 2 + a, 1), pl.ds(c, 1), :].reshape(1, tq)
        s = s + fq[a] - fk
        if diag:
            s = jnp.where(col <= row, s, NEG)
        return s

    out = _flash_pair(q_ref[...], load_k, load_v, bias_fn, i, i, s_scr, tq)
    o_ref[...] = out.astype(o_ref.dtype)


def _fox_attention(main, fcum, frow, batch, seq):
    tq = min(ATT_BLOCK, seq)
    nq = seq // tq
    n = batch * seq
    pairs = FOX_W // LANES
    return pl.pallas_call(
        functools.partial(_fox_kernel, tq=tq),
        grid=(batch, pairs, nq),
        in_specs=[pl.BlockSpec((tq, LANES), lambda b, h, i: (b * nq + i, h)),
                  pl.BlockSpec((seq, LANES), lambda b, h, i: (b, pairs + h)),
                  pl.BlockSpec((seq, LANES), lambda b, h, i: (b, 2 * pairs + h)),
                  pl.BlockSpec((tq, LANES), lambda b, h, i: (b * nq + i, 0)),
                  pl.BlockSpec((1, N_FOX_HEADS, nq, tq), lambda b, h, i: (b, 0, 0, 0))],
        out_specs=pl.BlockSpec((tq, LANES), lambda b, h, i: (b * nq + i, h)),
        out_shape=jax.ShapeDtypeStruct((n, FOX_W), BF16),
        scratch_shapes=[pltpu.VMEM((2, nq, tq, tq), F32)],
        compiler_params=_cparams(("parallel", "parallel", "arbitrary")),
    )(main, main, main, fcum, frow)


def _dsa_kernel(q_ref, k_ref, v_ref, iq_ref, ik_ref, iw_ref, o_ref,
                key_scr, bias_scr, kidx_scr, s_scr, *, tq, topk):
    i = pl.program_id(1)
    nch = i + 1

    @pl.when(i == 0)
    def _():
        kidx_scr[...] = ik_ref[...].astype(BF16)

    lane = lax.broadcasted_iota(jnp.int32, (tq, LANES), 1)
    first = lane < IDX_DIM
    row = lax.broadcasted_iota(jnp.int32, (tq, tq), 0)
    col = lax.broadcasted_iota(jnp.int32, (tq, tq), 1)
    iq = iq_ref[...]
    w = iw_ref[...] * ((N_IDX_HEADS * IDX_DIM) ** -0.5)
    qi = []
    for h in range(N_IDX_HEADS):
        grp = iq[:, (h // 2) * LANES:(h // 2 + 1) * LANES]
        zero = jnp.zeros_like(grp)
        qi.append(jnp.where(first, grp, zero) if h % 2 == 0 else jnp.where(first, zero, grp))

    w_t = jnp.transpose(w)
    wh = [w_t[N_FOX_HEADS + h:N_FOX_HEADS + h + 1, :] for h in range(N_IDX_HEADS)]

    def score_chunk(c, diag):
        start = pl.multiple_of(c * tq, tq)
        kc = kidx_scr[pl.ds(start, tq), :]
        sc = jnp.zeros((tq, tq), F32)
        for h in range(N_IDX_HEADS):
            lg = lax.dot_general(kc, qi[h], (((1,), (1,)), ((), ())),
                                 preferred_element_type=F32)
            sc = sc + wh[h] * jnp.maximum(lg, 0.0)
        if diag:
            sc = jnp.where(row <= col, sc, -jnp.inf)
        sc = jnp.where(sc == 0.0, 0.0, sc)
        bits = pltpu.bitcast(sc, jnp.int32)
        key_scr[c] = bits ^ ((bits >> 31) & 0x7FFFFFFF)

    def plain_body(c, carry):
        score_chunk(c, False)
        return carry

    lax.fori_loop(0, i, plain_body, 0)
    score_chunk(i, True)

    kf = float(topk)

    def count(pred):
        def body(c, part):
            ones = jnp.where(pred(key_scr[c]), 1.0, 0.0)
            return part + jnp.sum(ones.reshape(tq // 8, 8, tq), axis=0)
        part = lax.fori_loop(0, nch, body, jnp.zeros((8, tq), F32))
        return jnp.sum(part, axis=0, keepdims=True)

    def refine(cand, thr):
        return jnp.where(count(lambda k: k >= cand) >= kf, cand, thr)

    thr = jnp.full((1, tq), INT_MIN, jnp.int32)
    thr = refine(jnp.zeros((1, tq), jnp.int32), thr)
    thr = lax.fori_loop(
        0, 31, lambda it, t: refine(t | jnp.left_shift(jnp.int32(1), 30 - it), t), thr)

    need = kf - count(lambda k: k > thr)
    need = jnp.where(thr == KEY_NEG_INF, 0.0, need)
    earlier = jnp.where(col < row, 1.0, 0.0).astype(BF16)
    ones_rows = jnp.ones((8, tq), BF16)

    def sel_body(c, off):
        k = key_scr[c]
        eq = k == thr
        eqb = jnp.where(eq, 1.0, 0.0).astype(BF16)
        before = jnp.dot(earlier, eqb, preferred_element_type=F32) + off
        tie = jnp.where(before < need, 0.0, NEG)
        bias_t = jnp.where(k > thr, 0.0, jnp.where(eq, tie, NEG))
        bias_scr[c] = jnp.transpose(bias_t)
        return off + jnp.dot(ones_rows, eqb, preferred_element_type=F32)[0:1, :]

    lax.fori_loop(0, nch, sel_body, jnp.zeros((1, tq), F32))

    for g in range(DSA_W // LANES):
        cols = slice(g * LANES, (g + 1) * LANES)

        def load_k(c, cols=cols):
            return k_ref[pl.ds(pl.multiple_of(c * tq, tq), tq), cols]

        def load_v(c, cols=cols):
            return v_ref[pl.ds(pl.multiple_of(c * tq, tq), tq), cols]

        def bias_fn(a, c, s, diag):
            return s + bias_scr[c]

        out = _flash_pair(q_ref[:, cols], load_k, load_v, bias_fn, nch, None, s_scr, tq)
        o_ref[:, cols] = out.astype(o_ref.dtype)


def _dsa_attention(main, small, batch, seq):
    tq = min(ATT_BLOCK, seq)
    nq = seq // tq
    n = batch * seq
    topk = min(DSA_TOPK_MAX, seq // 4)
    assert topk <= tq
    iq_block = MAIN_COLS // 256 - 1
    return pl.pallas_call(
        functools.partial(_dsa_kernel, tq=tq, topk=topk),
        grid=(batch, nq),
        in_specs=[pl.BlockSpec((tq, DSA_W), lambda b, i: (b * nq + i, 3)),
                  pl.BlockSpec((seq, DSA_W), lambda b, i: (b, 4)),
                  pl.BlockSpec((seq, DSA_W), lambda b, i: (b, 5)),
                  pl.BlockSpec((tq, 256), lambda b, i: (b * nq + i, iq_block)),
                  pl.BlockSpec((seq, LANES), lambda b, i: (b, 0)),
                  pl.BlockSpec((tq, LANES), lambda b, i: (b * nq + i, 1))],
        out_specs=pl.BlockSpec((tq, DSA_W), lambda b, i: (b * nq + i, 0)),
        out_shape=jax.ShapeDtypeStruct((n, DSA_W), BF16),
        scratch_shapes=[pltpu.VMEM((nq, tq, tq), jnp.int32),
                        pltpu.VMEM((nq, tq, tq), F32),
                        pltpu.VMEM((seq, LANES), BF16),
                        pltpu.VMEM((2, nq, tq, tq), F32)],
        compiler_params=_cparams(("parallel", "arbitrary")),
    )(main, main, main, main, small, small)


def _matmul_ln_kernel(*refs, n_pairs):
    a_refs = refs[:n_pairs]
    w_refs = refs[n_pairs:2 * n_pairs]
    bias_ref, x_ref, g_ref, b_ref, o_ref, ob_ref = refs[2 * n_pairs:]
    acc = bias_ref[...] + ALPHA * x_ref[...]
    for a_ref, w_ref in zip(a_refs, w_refs):
        acc = acc + jnp.dot(a_ref[...], w_ref[...], preferred_element_type=F32)
    y = _layer_norm(acc, g_ref[...], b_ref[...])
    o_ref[...] = y
    ob_ref[...] = y.astype(BF16)


def _matmul_ln(a_list, w_list, bias, x, g, b, tm):
    n, d = x.shape
    n_pairs = len(a_list)
    row = lambda i: (i, 0)
    fixed = lambda i: (0, 0)
    in_specs = ([pl.BlockSpec((tm, a.shape[1]), row) for a in a_list]
                + [pl.BlockSpec(w.shape, fixed) for w in w_list]
                + [pl.BlockSpec((1, d), fixed), pl.BlockSpec((tm, d), row),
                   pl.BlockSpec((1, d), fixed), pl.BlockSpec((1, d), fixed)])
    return pl.pallas_call(
        functools.partial(_matmul_ln_kernel, n_pairs=n_pairs),
        grid=(n // tm,),
        in_specs=in_specs,
        out_specs=[pl.BlockSpec((tm, d), row), pl.BlockSpec((tm, d), row)],
        out_shape=[jax.ShapeDtypeStruct((n, d), F32), jax.ShapeDtypeStruct((n, d), BF16)],
        compiler_params=_cparams(("parallel",)),
    )(*a_list, *w_list, bias, x, g, b)


def _swiglu_kernel(xb_ref, wg_ref, wu_ref, wd_ref, x_ref, g_ref, b_ref, o_ref, ob_ref, acc_ref):
    f = pl.program_id(1)

    @pl.when(f == 0)
    def _():
        acc_ref[...] = ALPHA * x_ref[...]

    xb = xb_ref[...]
    hg = jnp.dot(xb, wg_ref[...], preferred_element_type=F32)
    hu = jnp.dot(xb, wu_ref[...], preferred_element_type=F32)
    h = hg * _sigmoid(hg) * hu
    acc_ref[...] += jnp.dot(h.astype(BF16), wd_ref[...], preferred_element_type=F32)

    @pl.when(f == pl.num_programs(1) - 1)
    def _():
        y = _layer_norm(acc_ref[...], g_ref[...], b_ref[...])
        o_ref[...] = y
        ob_ref[...] = y.astype(BF16)


def _swiglu_ln(xb, x, wg, wu, wd, g, b, tm, tf):
    n, d = x.shape
    ffn = wg.shape[1]
    row = lambda i, f: (i, 0)
    fixed = lambda i, f: (0, 0)
    return pl.pallas_call(
        _swiglu_kernel,
        grid=(n // tm, ffn // tf),
        in_specs=[pl.BlockSpec((tm, d), row),
                  pl.BlockSpec((d, tf), lambda i, f: (0, f)),
                  pl.BlockSpec((d, tf), lambda i, f: (0, f)),
                  pl.BlockSpec((tf, d), lambda i, f: (f, 0)),
                  pl.BlockSpec((tm, d), row),
                  pl.BlockSpec((1, d), fixed), pl.BlockSpec((1, d), fixed)],
        out_specs=[pl.BlockSpec((tm, d), row), pl.BlockSpec((tm, d), row)],
        out_shape=[jax.ShapeDtypeStruct((n, d), F32), jax.ShapeDtypeStruct((n, d), BF16)],
        scratch_shapes=[pltpu.VMEM((tm, d), F32)],
        compiler_params=_cparams(("parallel", "arbitrary")),
    )(xb, wg, wu, wd, x, g, b)


def _glu_kernel(x_ref, wa_ref, wg_ref, ba_ref, bg_ref, o_ref):
    xb = x_ref[...]
    a = jnp.dot(xb, wa_ref[...], preferred_element_type=F32) + ba_ref[...]
    gate = jnp.dot(xb, wg_ref[...], preferred_element_type=F32) + bg_ref[...]
    o_ref[...] = a * _sigmoid(gate)


def _glu_proj(xb, w, bias, tm, tn):
    n, d = xb.shape
    c = w.shape[1] // 2
    nc = c // tn
    return pl.pallas_call(
        _glu_kernel,
        grid=(n // tm, nc),
        in_specs=[pl.BlockSpec((tm, d), lambda i, j: (i, 0)),
                  pl.BlockSpec((d, tn), lambda i, j: (0, j)),
                  pl.BlockSpec((d, tn), lambda i, j: (0, nc + j)),
                  pl.BlockSpec((1, tn), lambda i, j: (0, j)),
                  pl.BlockSpec((1, tn), lambda i, j: (0, nc + j))],
        out_specs=pl.BlockSpec((tm, tn), lambda i, j: (i, j)),
        out_shape=jax.ShapeDtypeStruct((n, c), F32),
        compiler_params=_cparams(("parallel", "arbitrary")),
    )(xb, w, w, bias, bias)


def _dwconv_kernel(cur_ref, prev_ref, w_ref, bdw_ref, g_ref, b_ref, o_ref, buf_ref, sh_ref, y_ref,
                   *, ts, halo):
    i = pl.program_id(1)
    prev = prev_ref[...]
    buf_ref[0:halo, :] = jnp.where(i == 0, jnp.zeros_like(prev), prev)
    buf_ref[halo:halo + ts, :] = cur_ref[...]
    base = halo - (CONV_WIDTH - 1)
    span = sh_ref.shape[1]
    for cb in range(cur_ref.shape[1] // LANES):
        cols = slice(cb * LANES, (cb + 1) * LANES)
        for r in range(1, 8):
            sh_ref[r - 1] = buf_ref[r:r + span, cols]
        acc = jnp.zeros((ts, LANES), F32)
        for j in range(CONV_WIDTH):
            a, r = divmod(base + j, 8)
            if r == 0:
                win = buf_ref[8 * a:8 * a + ts, cols]
            else:
                win = sh_ref[r - 1, 8 * a:8 * a + ts, :]
            acc = acc + w_ref[j:j + 1, cols] * win
        y_ref[:, cols] = acc
    y = _layer_norm(y_ref[...] + bdw_ref[...], g_ref[...], b_ref[...])
    o_ref[...] = (y * _sigmoid(y)).astype(o_ref.dtype)


def _dwconv_ln_silu(h, w_dw, b_dw, g, b, batch, seq, ts):
    n, c = h.shape
    halo = 32
    nt = seq // ts
    ratio = ts // halo
    fixed = lambda bi, i: (0, 0)
    cur = lambda bi, i: (bi * nt + i, 0)
    prev = lambda bi, i: (jnp.maximum((bi * nt + i) * ratio - 1, 0), 0)
    return pl.pallas_call(
        functools.partial(_dwconv_kernel, ts=ts, halo=halo),
        grid=(batch, nt),
        in_specs=[pl.BlockSpec((ts, c), cur), pl.BlockSpec((halo, c), prev),
                  pl.BlockSpec((CONV_WIDTH, c), fixed),
                  pl.BlockSpec((1, c), fixed), pl.BlockSpec((1, c), fixed),
                  pl.BlockSpec((1, c), fixed)],
        out_specs=pl.BlockSpec((ts, c), cur),
        out_shape=jax.ShapeDtypeStruct((n, c), BF16),
        scratch_shapes=[pltpu.VMEM((halo + ts, c), F32),
                        pltpu.VMEM((7, ts + halo - 8, LANES), F32),
                        pltpu.VMEM((ts, c), F32)],
        compiler_params=_cparams(("parallel", "arbitrary")),
    )(h, h, w_dw, b_dw, g, b)


MOE_TILE = 512
MOE_EXPERT_TILE = 1024
NOT_ROUTED = -1e9


def _router_kernel(x_ref, w_ref, comb_ref, rank_ref, cum_ref, count_scr):
    @pl.when(pl.program_id(0) == 0)
    def _():
        count_scr[...] = jnp.zeros_like(count_scr)

    logits = jnp.dot(x_ref[...], w_ref[...], precision=lax.Precision.HIGHEST,
                     preferred_element_type=F32)
    tm = logits.shape[0]
    lane = lax.broadcasted_iota(jnp.int32, logits.shape, 1)
    logits = jnp.where(lane < N_EXPERTS, logits, -jnp.inf)
    m1 = jnp.max(logits, axis=1, keepdims=True)
    i1 = jnp.min(jnp.where(logits == m1, lane, LANES), axis=1, keepdims=True)
    rest = jnp.where(lane == i1, -jnp.inf, logits)
    m2 = jnp.max(rest, axis=1, keepdims=True)
    i2 = jnp.min(jnp.where(rest == m2, lane, LANES), axis=1, keepdims=True)
    e2 = jnp.exp(m2 - m1)
    g1 = 1.0 / (1.0 + e2)
    g2 = e2 / (1.0 + e2)
    comb_ref[...] = jnp.where(lane == i1, g1, 0.0) + jnp.where(lane == i2, g2, 0.0)

    routed = jnp.where(lane == i1, 1.0, jnp.where(lane == i2, 1.0, 0.0))
    r = lax.broadcasted_iota(jnp.int32, (tm, tm), 0)
    c = lax.broadcasted_iota(jnp.int32, (tm, tm), 1)
    earlier = jnp.where(c < r, 1.0, 0.0).astype(BF16)
    before = jnp.dot(earlier, routed.astype(BF16), preferred_element_type=F32)
    start = count_scr[0:1, :]
    rank_ref[...] = jnp.where(routed > 0.0, before + start, NOT_ROUTED)
    after = start + before[tm - 1:tm, :] + routed[tm - 1:tm, :]
    count_scr[...] = jnp.broadcast_to(after, count_scr.shape)
    cum_ref[...] = jnp.broadcast_to(after, cum_ref.shape)


def _router(x, w_router_padded, tm):
    n, d = x.shape
    row = lambda i: (i, 0)
    return pl.pallas_call(
        _router_kernel,
        grid=(n // tm,),
        in_specs=[pl.BlockSpec((tm, d), row),
                  pl.BlockSpec((d, LANES), lambda i: (0, 0))],
        out_specs=[pl.BlockSpec((tm, LANES), row), pl.BlockSpec((tm, LANES), row),
                   pl.BlockSpec((8, LANES), row)],
        out_shape=[jax.ShapeDtypeStruct((n, LANES), F32), jax.ShapeDtypeStruct((n, LANES), F32),
                   jax.ShapeDtypeStruct((n // tm * 8, LANES), F32)],
        scratch_shapes=[pltpu.VMEM((8, LANES), F32)],
        compiler_params=_cparams(("arbitrary",)),
    )(x, w_router_padded)


def _dispatch_kernel(g_idx, t_idx, valid, e_of_g, slot_ref, x_ref, comb_ref, xg_ref, gate_ref,
                     acc_ref, gacc_ref):
    i = pl.program_id(0)
    g = g_idx[i]
    first = jnp.logical_or(i == 0, g_idx[jnp.maximum(i - 1, 0)] != g)
    last = jnp.logical_or(i == pl.num_programs(0) - 1,
                          g_idx[jnp.minimum(i + 1, pl.num_programs(0) - 1)] != g)

    @pl.when(first)
    def _():
        acc_ref[...] = jnp.zeros_like(acc_ref)
        gacc_ref[...] = jnp.zeros_like(gacc_ref)

    @pl.when(valid[i] > 0)
    def _():
        t = acc_ref.shape[0]
        e = e_of_g[g]
        sub = lax.broadcasted_iota(jnp.int32, (8, t), 0)
        slot_row = jnp.sum(jnp.where(sub == e, slot_ref[...], 0.0), axis=0, keepdims=True)
        want = (g * t + lax.broadcasted_iota(jnp.int32, (t, t), 0)).astype(F32)
        sel = jnp.where(slot_row == want, 1.0, 0.0).astype(BF16)
        acc_ref[...] += jnp.dot(sel, x_ref[...], preferred_element_type=F32)
        comb = comb_ref[...]
        hi = comb.astype(BF16)
        lo = (comb - hi.astype(F32)).astype(BF16)
        gacc_ref[...] += (jnp.dot(sel, hi, preferred_element_type=F32)
                          + jnp.dot(sel, lo, preferred_element_type=F32))

    @pl.when(last)
    def _():
        xg_ref[...] = acc_ref[...].astype(xg_ref.dtype)
        gate_ref[...] = gacc_ref[...]


def _dispatch(work, e_of_g, slot_rows, xb, comb, n_slots):
    g_idx, t_idx, valid = work
    n, d = xb.shape
    t = MOE_TILE
    tok = lambda i, g, ti, v, e: (ti[i], 0)
    slot = lambda i, g, ti, v, e: (g[i], 0)
    return pl.pallas_call(
        _dispatch_kernel,
        grid_spec=pltpu.PrefetchScalarGridSpec(
            num_scalar_prefetch=4,
            grid=(g_idx.shape[0],),
            in_specs=[pl.BlockSpec((8, t), lambda i, g, ti, v, e: (0, ti[i])),
                      pl.BlockSpec((t, d), tok), pl.BlockSpec((t, LANES), tok)],
            out_specs=[pl.BlockSpec((t, d), slot), pl.BlockSpec((t, LANES), slot)],
            scratch_shapes=[pltpu.VMEM((t, d), F32), pltpu.VMEM((t, LANES), F32)]),
        out_shape=[jax.ShapeDtypeStruct((n_slots, d), BF16),
                   jax.ShapeDtypeStruct((n_slots, LANES), F32)],
        compiler_params=_cparams(("arbitrary",)),
    )(g_idx, t_idx, valid, e_of_g, slot_rows, xb, comb)


def _expert_kernel(e_of_big, nvalid, xg_ref, gate_ref, wg_ref, wu_ref, wd_ref, y_ref, acc_ref):
    b = pl.program_id(0)
    f = pl.program_id(1)
    nv = nvalid[b]

    @pl.when(nv > 0)
    def _():
        @pl.when(f == 0)
        def _():
            acc_ref[...] = jnp.zeros_like(acc_ref)

        tm = acc_ref.shape[0]
        e = e_of_big[b]
        live = lax.broadcasted_iota(jnp.int32, (tm, 1), 0) < nv
        xg = xg_ref[...]
        xg = jnp.where(live, xg, jnp.zeros_like(xg))
        gates = gate_ref[...]
        lane = lax.broadcasted_iota(jnp.int32, gates.shape, 1)
        ge = jnp.sum(jnp.where(jnp.logical_and(lane == e, live), gates, 0.0),
                     axis=1, keepdims=True)
        hg = jnp.dot(xg, wg_ref[...], preferred_element_type=F32)
        hu = jnp.dot(xg, wu_ref[...], preferred_element_type=F32)
        h = hg * _sigmoid(hg) * hu * ge
        acc_ref[...] += jnp.dot(h.astype(BF16), wd_ref[...], preferred_element_type=F32)

        @pl.when(f == pl.num_programs(1) - 1)
        def _():
            y_ref[...] = acc_ref[...].astype(y_ref.dtype)


def _expert_ffn(e_of_big, nvalid, xg, gate, wg, wu, wd, tf):
    n_slots, d = xg.shape
    edim = wg.shape[2]
    tm = MOE_EXPERT_TILE
    row = lambda b, f, e, nv: (b, 0)
    return pl.pallas_call(
        _expert_kernel,
        grid_spec=pltpu.PrefetchScalarGridSpec(
            num_scalar_prefetch=2,
            grid=(n_slots // tm, edim // tf),
            in_specs=[pl.BlockSpec((tm, d), row), pl.BlockSpec((tm, LANES), row),
                      pl.BlockSpec((None, d, tf), lambda b, f, e, nv: (e[b], 0, f)),
                      pl.BlockSpec((None, d, tf), lambda b, f, e, nv: (e[b], 0, f)),
                      pl.BlockSpec((None, tf, d), lambda b, f, e, nv: (e[b], f, 0))],
            out_specs=pl.BlockSpec((tm, d), row),
            scratch_shapes=[pltpu.VMEM((tm, d), F32)]),
        out_shape=jax.ShapeDtypeStruct((n_slots, d), BF16),
        compiler_params=_cparams(("parallel", "arbitrary")),
    )(e_of_big, nvalid, xg, gate, wg, wu, wd)


def _combine_kernel(t_idx, g_idx, valid, e_of_g, slot_ref, y_ref, x_ref, g_ref, b_ref,
                    o_ref, ob_ref, acc_ref):
    i = pl.program_id(0)
    ti = t_idx[i]
    first = jnp.logical_or(i == 0, t_idx[jnp.maximum(i - 1, 0)] != ti)
    last = jnp.logical_or(i == pl.num_programs(0) - 1,
                          t_idx[jnp.minimum(i + 1, pl.num_programs(0) - 1)] != ti)

    @pl.when(first)
    def _():
        acc_ref[...] = ALPHA * x_ref[...]

    @pl.when(valid[i] > 0)
    def _():
        t = acc_ref.shape[0]
        g = g_idx[i]
        e = e_of_g[g]
        slots = slot_ref[...]
        lane = lax.broadcasted_iota(jnp.int32, slots.shape, 1)
        slot_col = jnp.sum(jnp.where(lane == e, slots, 0.0), axis=1, keepdims=True)
        want = (g * t + lax.broadcasted_iota(jnp.int32, (t, t), 1)).astype(F32)
        sel = jnp.where(slot_col == want, 1.0, 0.0).astype(BF16)
        acc_ref[...] += jnp.dot(sel, y_ref[...], preferred_element_type=F32)

    @pl.when(last)
    def _():
        y = _layer_norm(acc_ref[...], g_ref[...], b_ref[...])
        o_ref[...] = y
        ob_ref[...] = y.astype(BF16)


def _combine_ln(work, e_of_g, slot_cols, y, x, g, b):
    t_idx, g_idx, valid = work
    n, d = x.shape
    t = MOE_TILE
    tok = lambda i, ti, gi, v, e: (ti[i], 0)
    fixed = lambda i, ti, gi, v, e: (0, 0)
    return pl.pallas_call(
        _combine_kernel,
        grid_spec=pltpu.PrefetchScalarGridSpec(
            num_scalar_prefetch=4,
            grid=(t_idx.shape[0],),
            in_specs=[pl.BlockSpec((t, LANES), tok),
                      pl.BlockSpec((t, d), lambda i, ti, gi, v, e: (gi[i], 0)),
                      pl.BlockSpec((t, d), tok),
                      pl.BlockSpec((1, d), fixed), pl.BlockSpec((1, d), fixed)],
            out_specs=[pl.BlockSpec((t, d), tok), pl.BlockSpec((t, d), tok)],
            scratch_shapes=[pltpu.VMEM((t, d), F32)]),
        out_shape=[jax.ShapeDtypeStruct((n, d), F32), jax.ShapeDtypeStruct((n, d), BF16)],
        compiler_params=_cparams(("arbitrary",)),
    )(t_idx, g_idx, valid, e_of_g, slot_cols, y, x, g, b)


def _routing_tables(rank, cum, n):
    t, te = MOE_TILE, MOE_EXPERT_TILE
    nt = n // t
    i32 = jnp.int32
    after = cum.reshape(nt, 8, LANES)[:, 0, :N_EXPERTS].astype(i32)
    before = jnp.concatenate([jnp.zeros((1, N_EXPERTS), i32), after[:-1]], axis=0)
    totals = after[-1]
    padded = (totals + te - 1) // te * te
    off = jnp.cumsum(padded) - padded
    n_slots = 2 * n + N_EXPERTS * te
    n_big = n_slots // te
    n_g = n_slots // t

    big_start = jnp.arange(n_big, dtype=i32) * te
    e_of_big = jnp.minimum(jnp.sum(big_start[:, None] >= (off + padded)[None, :], axis=1),
                           N_EXPERTS - 1).astype(i32)
    nvalid = jnp.clip(totals[e_of_big] - (big_start - off[e_of_big]), 0, te).astype(i32)
    e_of_g = jnp.repeat(e_of_big, te // t)

    off_row = jnp.concatenate([off.astype(F32), jnp.zeros((LANES - N_EXPERTS,), F32)])[None, :]
    slot_cols = rank + off_row
    slot_rows = jnp.transpose(slot_cols[:, :N_EXPERTS])

    has = after > before
    g0 = (off[None, :] + before) // t
    g1 = (off[None, :] + jnp.maximum(after, 1) - 1) // t
    t_ids = jnp.broadcast_to(jnp.arange(nt, dtype=i32)[:, None], (nt, N_EXPERTS))
    cand_g = jnp.stack([g0, g0 + 1], axis=-1).reshape(-1).astype(i32)
    cand_t = jnp.stack([t_ids, t_ids], axis=-1).reshape(-1)
    cand_ok = jnp.stack([has, jnp.logical_and(has, g1 > g0)], axis=-1).reshape(-1)
    n_items = min(N_EXPERTS * nt + n_g, 2 * N_EXPERTS * nt)
    count = jnp.sum(cand_ok.astype(i32))
    big = jnp.int32(2 ** 30)

    def ordered(major, minor, minor_range):
        order = jnp.argsort(jnp.where(cand_ok, major * minor_range + minor, big))[:n_items]
        pos = jnp.minimum(jnp.arange(n_items, dtype=i32), count - 1)
        pick = order[pos]
        ok = (jnp.arange(n_items, dtype=i32) < count).astype(i32)
        return cand_g[pick], cand_t[pick], ok

    dg, dt, dok = ordered(cand_g, cand_t, nt)
    cg, ct, cok = ordered(cand_t, cand_g, n_g)
    return dict(n_slots=n_slots, e_of_big=e_of_big, nvalid=nvalid, e_of_g=e_of_g,
                slot_cols=slot_cols, slot_rows=slot_rows,
                dispatch=(dg, dt, dok), combine=(ct, cg, cok))


def _moe_ln(xb, x, w_router_padded, wg, wu, wd, g, b, tf):
    n, _ = x.shape
    comb, rank, cum = _router(x, w_router_padded, MOE_TILE)
    rt = _routing_tables(rank, cum, n)
    xg, gate = _dispatch(rt["dispatch"], rt["e_of_g"], rt["slot_rows"], xb, comb, rt["n_slots"])
    y = _expert_ffn(rt["e_of_big"], rt["nvalid"], xg, gate, wg, wu, wd, tf)
    return _combine_ln(rt["combine"], rt["e_of_g"], rt["slot_cols"], y, x, g, b)


def _rope_patterns(seq):
    half = ROPE_DIM // 2
    inv_freq = ROPE_THETA ** (-jnp.arange(0, ROPE_DIM, 2, dtype=F32) / ROPE_DIM)
    ang = jnp.arange(seq, dtype=F32)[:, None] * inv_freq[None, :]
    cos, sin = jnp.cos(ang), jnp.sin(ang)
    rest = HEAD_DIM - ROPE_DIM
    c = jnp.concatenate([cos, cos, jnp.ones((seq, rest), F32)], axis=1)
    s1 = jnp.concatenate([-sin, jnp.zeros((seq, half + rest), F32)], axis=1)
    s2 = jnp.concatenate([jnp.zeros((seq, half), F32), sin, jnp.zeros((seq, rest), F32)], axis=1)
    return c, s1, s2


def _row_tile(n, want):
    return want if n % want == 0 else n


def _attention_layer(x, xb, w_in, b_f, w_out, g, b, tabs, batch, seq):
    n = batch * seq
    o = [0, 512, 1024, 1536, 1544, 2056, 2568, 3080, 3336, 3400, 3404]
    part = lambda k: w_in[:, o[k]:o[k + 1]]
    scale = HEAD_DIM ** -0.5
    pad = jnp.zeros((D_MODEL, LANES - N_FOX_HEADS - N_IDX_HEADS), F32)
    w_all = jnp.concatenate([part(0) * scale, part(1), part(2), part(4) * scale, part(5),
                             part(6), part(7), part(8), part(8), part(3), part(9), pad],
                            axis=1).astype(BF16)
    main, small = _project(xb, w_all, tabs, seq, _row_tile(seq, 512))
    bias_row = jnp.concatenate([b_f.astype(F32), jnp.zeros((LANES - N_FOX_HEADS,), F32)])[None, :]
    fcum = _forget_cumsum(small, bias_row, batch, seq)
    tq = min(ATT_BLOCK, seq)
    frow = jnp.transpose(fcum.reshape(batch, seq, LANES)[:, :, :N_FOX_HEADS], (0, 2, 1))
    frow = frow.reshape(batch, N_FOX_HEADS, seq // tq, tq)
    fox = _fox_attention(main, fcum, frow, batch, seq)
    dsa = _dsa_attention(main, small, batch, seq)
    wo = w_out.astype(BF16)
    zero_bias = jnp.zeros((1, D_MODEL), F32)
    return _matmul_ln([fox, dsa], [wo[:FOX_W], wo[FOX_W:]], zero_bias, x, g, b,
                      _row_tile(n, 512))


def kernel(x, norm_mix_g, norm_mix_b, norm_ffn_g, norm_ffn_b, attn_w_in, attn_b_f, attn_w_out,
           ffn_w_gate, ffn_w_up, ffn_w_down, conv_w_in, conv_b_in, conv_w_dw, conv_b_dw,
           conv_ln_g, conv_ln_b, conv_w_out, conv_b_out, moe_w_router, moe_w_gate, moe_w_up,
           moe_w_down):
    batch, seq, d = x.shape
    n = batch * seq
    tabs = tuple(jnp.tile(p, (1, MAIN_TN // HEAD_DIM)) for p in _rope_patterns(seq))
    r2 = lambda v: v.astype(F32).reshape(1, -1)

    xf = x.reshape(n, d)
    xb = xf.astype(BF16)
    for layer in range(DEPTH):
        j = layer // 2
        gm, bm = r2(norm_mix_g[layer]), r2(norm_mix_b[layer])
        gf, bf = r2(norm_ffn_g[layer]), r2(norm_ffn_b[layer])
        if layer % 2 == 0:
            xf, xb = _attention_layer(xf, xb, attn_w_in[j], attn_b_f[j], attn_w_out[j], gm, bm,
                                      tabs, batch, seq)
            xf, xb = _swiglu_ln(xb, xf, ffn_w_gate[j].astype(BF16), ffn_w_up[j].astype(BF16),
                                ffn_w_down[j].astype(BF16), gf, bf, _row_tile(n, 512), 1408)
        else:
            h = _glu_proj(xb, conv_w_in[j].astype(BF16), r2(conv_b_in[j]), _row_tile(n, 1024), 512)
            hb = _dwconv_ln_silu(h, conv_w_dw[j].astype(F32), r2(conv_b_dw[j]),
                                 r2(conv_ln_g[j]), r2(conv_ln_b[j]), batch, seq,
                                 min(256, seq))
            xf, xb = _matmul_ln([hb], [conv_w_out[j].astype(BF16)], r2(conv_b_out[j]), xf, gm, bm,
                                _row_tile(n, 512))
            w_r = jnp.concatenate([moe_w_router[j].astype(F32),
                                   jnp.zeros((d, LANES - N_EXPERTS), F32)], axis=1)
            xf, xb = _moe_ln(xb, xf, w_r, moe_w_gate[j].astype(BF16), moe_w_up[j].astype(BF16),
                             moe_w_down[j].astype(BF16), gf, bf, 512)
    return xf.reshape(batch, seq, d)
```

```python
import functools

import jax
import jax.numpy as jnp
from jax import lax
from jax.experimental import pallas as pl
from jax.experimental.pallas import tpu as pltpu

F32 = jnp.float32
BF16 = jnp.bfloat16

D_MODEL = 1024
DEPTH = 4
HEAD_DIM = 64
FOX_W = 512
DSA_W = 512
N_FOX_HEADS = 8
N_IDX_HEADS = 4
IDX_DIM = 64
DSA_TOPK_MAX = 256
ROPE_DIM = 16
ROPE_THETA = 500000.0
CONV_WIDTH = 31
N_EXPERTS = 8
LN_EPS = 1e-5
ALPHA = (2.0 * DEPTH) ** 0.25

LANES = 128
NEG = -1e30
INT_MIN = -2 ** 31
KEY_NEG_INF = -2 ** 31 + 0x007FFFFF
VMEM_LIMIT = 56 * 1024 * 1024

ATT_BLOCK = 512
MAIN_COLS = 3328
MAIN_TN = 256
ROPE_TILES = (6, 7, 8, 9, 12)


def _cparams(sem):
    return pltpu.CompilerParams(dimension_semantics=sem, vmem_limit_bytes=VMEM_LIMIT)


def _layer_norm(y, g, b):
    mu = jnp.mean(y, axis=-1, keepdims=True)
    d = y - mu
    var = jnp.mean(d * d, axis=-1, keepdims=True)
    return d * lax.rsqrt(var + LN_EPS) * g + b


def _sigmoid(z):
    return 1.0 / (1.0 + jnp.exp(-z))


def _rope(acc, c, s1, s2):
    tn = acc.shape[1]
    return acc * c + pltpu.roll(acc, tn - 8, 1) * s1 + pltpu.roll(acc, 8, 1) * s2


def _proj_kernel(x_ref, w_ref, c_ref, s1_ref, s2_ref, main_ref, small_ref):
    x = x_ref[...]
    for j in range(MAIN_COLS // MAIN_TN):
        cols = slice(j * MAIN_TN, (j + 1) * MAIN_TN)
        acc = jnp.dot(x, w_ref[:, cols], preferred_element_type=F32)
        if j in ROPE_TILES:
            acc = _rope(acc, c_ref[...], s1_ref[...], s2_ref[...])
        main_ref[:, cols] = acc.astype(main_ref.dtype)
    acc = jnp.dot(x, w_ref[:, MAIN_COLS:MAIN_COLS + 2 * LANES], preferred_element_type=F32)
    small_ref[:, :LANES] = _rope(acc[:, :LANES], c_ref[:, :LANES], s1_ref[:, :LANES],
                                 s2_ref[:, :LANES])
    small_ref[:, LANES:] = acc[:, LANES:]


def _project(xb, w, tables, seq, tm):
    n, d = xb.shape
    pos_blocks = seq // tm
    tab_spec = pl.BlockSpec((tm, MAIN_TN), lambda i: (i % pos_blocks, 0))
    row = lambda i: (i, 0)
    return pl.pallas_call(
        _proj_kernel,
        grid=(n // tm,),
        in_specs=[pl.BlockSpec((tm, d), row),
                  pl.BlockSpec(w.shape, lambda i: (0, 0)),
                  tab_spec, tab_spec, tab_spec],
        out_specs=[pl.BlockSpec((tm, MAIN_COLS), row), pl.BlockSpec((tm, 2 * LANES), row)],
        out_shape=[jax.ShapeDtypeStruct((n, MAIN_COLS), BF16),
                   jax.ShapeDtypeStruct((n, 2 * LANES), F32)],
        compiler_params=_cparams(("parallel",)),
    )(xb, w, *tables)


def _forget_kernel(z_ref, b_ref, o_ref, *, chunk):
    s = z_ref.shape[0]
    z = z_ref[...] + b_ref[...]
    logf = jnp.minimum(z, 0.0) - jnp.log(1.0 + jnp.exp(-jnp.abs(z)))
    r = lax.broadcasted_iota(jnp.int32, (chunk, chunk), 0)
    c = lax.broadcasted_iota(jnp.int32, (chunk, chunk), 1)
    tri = jnp.where(c <= r, 1.0, 0.0).astype(F32)
    carry = jnp.zeros((1, LANES), F32)
    for k in range(s // chunk):
        blk = logf[k * chunk:(k + 1) * chunk, :]
        cs = jnp.dot(tri, blk, precision=lax.Precision.HIGHEST,
                     preferred_element_type=F32) + carry
        o_ref[k * chunk:(k + 1) * chunk, :] = cs
        carry = cs[chunk - 1:chunk, :]


def _forget_cumsum(small, bias_row, batch, seq):
    chunk = min(256, seq)
    return pl.pallas_call(
        functools.partial(_forget_kernel, chunk=chunk),
        grid=(batch,),
        in_specs=[pl.BlockSpec((seq, LANES), lambda b: (b, 1)),
                  pl.BlockSpec((1, LANES), lambda b: (0, 0))],
        out_specs=pl.BlockSpec((seq, LANES), lambda b: (b, 0)),
        out_shape=jax.ShapeDtypeStruct((batch * seq, LANES), F32),
        compiler_params=_cparams(("parallel",)),
    )(small, bias_row)


def _flash_pair(q2, load_k, load_v, bias_fn, n_plain, diag_chunk, s_scr, tq):
    lane = lax.broadcasted_iota(jnp.int32, (tq, LANES), 1)
    first = lane < HEAD_DIM
    qa = (jnp.where(first, q2, jnp.zeros_like(q2)), jnp.where(first, jnp.zeros_like(q2), q2))
    folds = s_scr.shape[3] // LANES

    def score(c, mx, diag):
        kc = load_k(c)
        new = []
        for a in range(2):
            s = lax.dot_general(qa[a], kc, (((1,), (1,)), ((), ())),
                                preferred_element_type=F32)
            s = bias_fn(a, c, s, diag)
            s_scr[a, c] = s
            m = mx[a]
            for f in range(folds):
                m = jnp.maximum(m, s[:, f * LANES:(f + 1) * LANES])
            new.append(m)
        return tuple(new)

    mx = tuple(jnp.full((tq, LANES), NEG, F32) for _ in range(2))
    mx = lax.fori_loop(0, n_plain, lambda c, m: score(c, m, False), mx)
    n_all = n_plain
    if diag_chunk is not None:
        mx = score(diag_chunk, mx, True)
        n_all = n_plain + 1
    row_max = [jnp.max(m, axis=1, keepdims=True) for m in mx]

    def weigh(c, carry):
        vc = load_v(c)
        one = jnp.ones_like(vc)
        vs = (jnp.where(first_k, vc, one), jnp.where(first_k, one, vc))
        new = []
        for a in range(2):
            p = jnp.exp(s_scr[a, c] - row_max[a])
            new.append(carry[a] + jnp.dot(p.astype(BF16), vs[a], preferred_element_type=F32))
        return tuple(new)

    first_k = lax.broadcasted_iota(jnp.int32, (s_scr.shape[3], LANES), 1) < HEAD_DIM
    init = tuple(jnp.zeros((tq, LANES), F32) for _ in range(2))
    acc0, acc1 = lax.fori_loop(0, n_all, weigh, init)
    o0 = acc0 / acc0[:, HEAD_DIM:HEAD_DIM + 1]
    o1 = acc1 / acc1[:, 0:1]
    return jnp.where(first, o0, o1)


def _fox_kernel(q_ref, k_ref, v_ref, fc_ref, fr_ref, o_ref, s_scr, *, tq):
    hp = pl.program_id(1)
    i = pl.program_id(2)
    lane = lax.broadcasted_iota(jnp.int32, (tq, LANES), 1)
    fc = fc_ref[...]
    fq = [jnp.sum(jnp.where(lane == hp * 2 + a, fc, 0.0), axis=1, keepdims=True)
          for a in range(2)]
    row = lax.broadcasted_iota(jnp.int32, (tq, tq), 0)
    col = lax.broadcasted_iota(jnp.int32, (tq, tq), 1)

    def load_k(c):
        return k_ref[pl.ds(pl.multiple_of(c * tq, tq), tq), :]

    def load_v(c):
        return v_ref[pl.ds(pl.multiple_of(c * tq, tq), tq), :]

    def bias_fn(a, c, s, diag):
        fk = fr_ref[0, pl.ds(hp * 2 + a, 1), pl.ds(c, 1), :].reshape(1, tq)
        s = s + fq[a] - fk
        if diag:
            s = jnp.where(col <= row, s, NEG)
        return s

    out = _flash_pair(q_ref[...], load_k, load_v, bias_fn, i, i, s_scr, tq)
    o_ref[...] = out.astype(o_ref.dtype)


def _fox_attention(main, fcum, frow, batch, seq):
    tq = min(ATT_BLOCK, seq)
    nq = seq // tq
    n = batch * seq
    pairs = FOX_W // LANES
    return pl.pallas_call(
        functools.partial(_fox_kernel, tq=tq),
        grid=(batch, pairs, nq),
        in_specs=[pl.BlockSpec((tq, LANES), lambda b, h, i: (b * nq + i, h)),
                  pl.BlockSpec((seq, LANES), lambda b, h, i: (b, pairs + h)),
                  pl.BlockSpec((seq, LANES), lambda b, h, i: (b, 2 * pairs + h)),
                  pl.BlockSpec((tq, LANES), lambda b, h, i: (b * nq + i, 0)),
                  pl.BlockSpec((1, N_FOX_HEADS, nq, tq), lambda b, h, i: (b, 0, 0, 0))],
        out_specs=pl.BlockSpec((tq, LANES), lambda b, h, i: (b * nq + i, h)),
        out_shape=jax.ShapeDtypeStruct((n, FOX_W), BF16),
        scratch_shapes=[pltpu.VMEM((2, nq, tq, tq), F32)],
        compiler_params=_cparams(("parallel", "parallel", "arbitrary")),
    )(main, main, main, fcum, frow)


def _dsa_kernel(q_ref, k_ref, v_ref, iq_ref, ik_ref, iw_ref, o_ref,
                key_scr, bias_scr, kidx_scr, s_scr, *, tq, topk):
    i = pl.program_id(1)
    nch = i + 1

    @pl.when(i == 0)
    def _():
        kidx_scr[...] = ik_ref[...].astype(BF16)

    lane = lax.broadcasted_iota(jnp.int32, (tq, LANES), 1)
    first = lane < IDX_DIM
    row = lax.broadcasted_iota(jnp.int32, (tq, tq), 0)
    col = lax.broadcasted_iota(jnp.int32, (tq, tq), 1)
    iq = iq_ref[...]
    w = iw_ref[...] * ((N_IDX_HEADS * IDX_DIM) ** -0.5)
    qi = []
    for h in range(N_IDX_HEADS):
        grp = iq[:, (h // 2) * LANES:(h // 2 + 1) * LANES]
        zero = jnp.zeros_like(grp)
        qi.append(jnp.where(first, grp, zero) if h % 2 == 0 else jnp.where(first, zero, grp))

    w_t = jnp.transpose(w)
    wh = [w_t[N_FOX_HEADS + h:N_FOX_HEADS + h + 1, :] for h in range(N_IDX_HEADS)]

    def score_chunk(c, diag):
        start = pl.multiple_of(c * tq, tq)
        kc = kidx_scr[pl.ds(start, tq), :]
        sc = jnp.zeros((tq, tq), F32)
        for h in range(N_IDX_HEADS):
            lg = lax.dot_general(kc, qi[h], (((1,), (1,)), ((), ())),
                                 preferred_element_type=F32)
            sc = sc + wh[h] * jnp.maximum(lg, 0.0)
        if diag:
            sc = jnp.where(row <= col, sc, -jnp.inf)
        sc = jnp.where(sc == 0.0, 0.0, sc)
        bits = pltpu.bitcast(sc, jnp.int32)
        key_scr[c] = bits ^ ((bits >> 31) & 0x7FFFFFFF)

    def plain_body(c, carry):
        score_chunk(c, False)
        return carry

    lax.fori_loop(0, i, plain_body, 0)
    score_chunk(i, True)

    kf = float(topk)

    def count(pred):
        def body(c, part):
            ones = jnp.where(pred(key_scr[c]), 1.0, 0.0)
            return part + jnp.sum(ones.reshape(tq // 8, 8, tq), axis=0)
        part = lax.fori_loop(0, nch, body, jnp.zeros((8, tq), F32))
        return jnp.sum(part, axis=0, keepdims=True)

    def refine(cand, thr):
        return jnp.where(count(lambda k: k >= cand) >= kf, cand, thr)

    thr = jnp.full((1, tq), INT_MIN, jnp.int32)
    thr = refine(jnp.zeros((1, tq), jnp.int32), thr)
    thr = lax.fori_loop(
        0, 31, lambda it, t: refine(t | jnp.left_shift(jnp.int32(1), 30 - it), t), thr)

    need = kf - count(lambda k: k > thr)
    need = jnp.where(thr == KEY_NEG_INF, 0.0, need)
    earlier = jnp.where(col < row, 1.0, 0.0).astype(BF16)
    ones_rows = jnp.ones((8, tq), BF16)

    def sel_body(c, off):
        k = key_scr[c]
        eq = k == thr
        eqb = jnp.where(eq, 1.0, 0.0).astype(BF16)
        before = jnp.dot(earlier, eqb, preferred_element_type=F32) + off
        tie = jnp.where(before < need, 0.0, NEG)
        bias_t = jnp.where(k > thr, 0.0, jnp.where(eq, tie, NEG))
        bias_scr[c] = jnp.transpose(bias_t)
        return off + jnp.dot(ones_rows, eqb, preferred_element_type=F32)[0:1, :]

    lax.fori_loop(0, nch, sel_body, jnp.zeros((1, tq), F32))

    for g in range(DSA_W // LANES):
        cols = slice(g * LANES, (g + 1) * LANES)

        def load_k(c, cols=cols):
            return k_ref[pl.ds(pl.multiple_of(c * tq, tq), tq), cols]

        def load_v(c, cols=cols):
            return v_ref[pl.ds(pl.multiple_of(c * tq, tq), tq), cols]

        def bias_fn(a, c, s, diag):
            return s + bias_scr[c]

        out = _flash_pair(q_ref[:, cols], load_k, load_v, bias_fn, nch, None, s_scr, tq)
        o_ref[:, cols] = out.astype(o_ref.dtype)


def _dsa_attention(main, small, batch, seq):
    tq = min(ATT_BLOCK, seq)
    nq = seq // tq
    n = batch * seq
    topk = min(DSA_TOPK_MAX, seq // 4)
    assert topk <= tq
    iq_block = MAIN_COLS // 256 - 1
    return pl.pallas_call(
        functools.partial(_dsa_kernel, tq=tq, topk=topk),
        grid=(batch, nq),
        in_specs=[pl.BlockSpec((tq, DSA_W), lambda b, i: (b * nq + i, 3)),
                  pl.BlockSpec((seq, DSA_W), lambda b, i: (b, 4)),
                  pl.BlockSpec((seq, DSA_W), lambda b, i: (b, 5)),
                  pl.BlockSpec((tq, 256), lambda b, i: (b * nq + i, iq_block)),
                  pl.BlockSpec((seq, LANES), lambda b, i: (b, 0)),
                  pl.BlockSpec((tq, LANES), lambda b, i: (b * nq + i, 1))],
        out_specs=pl.BlockSpec((tq, DSA_W), lambda b, i: (b * nq + i, 0)),
        out_shape=jax.ShapeDtypeStruct((n, DSA_W), BF16),
        scratch_shapes=[pltpu.VMEM((nq, tq, tq), jnp.int32),
                        pltpu.VMEM((nq, tq, tq), F32),
                        pltpu.VMEM((seq, LANES), BF16),
                        pltpu.VMEM((2, nq, tq, tq), F32)],
        compiler_params=_cparams(("parallel", "arbitrary")),
    )(main, main, main, main, small, small)


def _matmul_ln_kernel(*refs, n_pairs):
    a_refs = refs[:n_pairs]
    w_refs = refs[n_pairs:2 * n_pairs]
    bias_ref, x_ref, g_ref, b_ref, o_ref, ob_ref = refs[2 * n_pairs:]
    acc = bias_ref[...] + ALPHA * x_ref[...]
    for a_ref, w_ref in zip(a_refs, w_refs):
        acc = acc + jnp.dot(a_ref[...], w_ref[...], preferred_element_type=F32)
    y = _layer_norm(acc, g_ref[...], b_ref[...])
    o_ref[...] = y
    ob_ref[...] = y.astype(BF16)


def _matmul_ln(a_list, w_list, bias, x, g, b, tm):
    n, d = x.shape
    n_pairs = len(a_list)
    row = lambda i: (i, 0)
    fixed = lambda i: (0, 0)
    in_specs = ([pl.BlockSpec((tm, a.shape[1]), row) for a in a_list]
                + [pl.BlockSpec(w.shape, fixed) for w in w_list]
                + [pl.BlockSpec((1, d), fixed), pl.BlockSpec((tm, d), row),
                   pl.BlockSpec((1, d), fixed), pl.BlockSpec((1, d), fixed)])
    return pl.pallas_call(
        functools.partial(_matmul_ln_kernel, n_pairs=n_pairs),
        grid=(n // tm,),
        in_specs=in_specs,
        out_specs=[pl.BlockSpec((tm, d), row), pl.BlockSpec((tm, d), row)],
        out_shape=[jax.ShapeDtypeStruct((n, d), F32), jax.ShapeDtypeStruct((n, d), BF16)],
        compiler_params=_cparams(("parallel",)),
    )(*a_list, *w_list, bias, x, g, b)


def _swiglu_kernel(xb_ref, wg_ref, wu_ref, wd_ref, x_ref, g_ref, b_ref, o_ref, ob_ref, acc_ref):
    f = pl.program_id(1)

    @pl.when(f == 0)
    def _():
        acc_ref[...] = ALPHA * x_ref[...]

    xb = xb_ref[...]
    hg = jnp.dot(xb, wg_ref[...], preferred_element_type=F32)
    hu = jnp.dot(xb, wu_ref[...], preferred_element_type=F32)
    h = hg * _sigmoid(hg) * hu
    acc_ref[...] += jnp.dot(h.astype(BF16), wd_ref[...], preferred_element_type=F32)

    @pl.when(f == pl.num_programs(1) - 1)
    def _():
        y = _layer_norm(acc_ref[...], g_ref[...], b_ref[...])
        o_ref[...] = y
        ob_ref[...] = y.astype(BF16)


def _swiglu_ln(xb, x, wg, wu, wd, g, b, tm, tf):
    n, d = x.shape
    ffn = wg.shape[1]
    row = lambda i, f: (i, 0)
    fixed = lambda i, f: (0, 0)
    return pl.pallas_call(
        _swiglu_kernel,
        grid=(n // tm, ffn // tf),
        in_specs=[pl.BlockSpec((tm, d), row),
                  pl.BlockSpec((d, tf), lambda i, f: (0, f)),
                  pl.BlockSpec((d, tf), lambda i, f: (0, f)),
                  pl.BlockSpec((tf, d), lambda i, f: (f, 0)),
                  pl.BlockSpec((tm, d), row),
                  pl.BlockSpec((1, d), fixed), pl.BlockSpec((1, d), fixed)],
        out_specs=[pl.BlockSpec((tm, d), row), pl.BlockSpec((tm, d), row)],
        out_shape=[jax.ShapeDtypeStruct((n, d), F32), jax.ShapeDtypeStruct((n, d), BF16)],
        scratch_shapes=[pltpu.VMEM((tm, d), F32)],
        compiler_params=_cparams(("parallel", "arbitrary")),
    )(xb, wg, wu, wd, x, g, b)


def _glu_kernel(x_ref, wa_ref, wg_ref, ba_ref, bg_ref, o_ref):
    xb = x_ref[...]
    a = jnp.dot(xb, wa_ref[...], preferred_element_type=F32) + ba_ref[...]
    gate = jnp.dot(xb, wg_ref[...], preferred_element_type=F32) + bg_ref[...]
    o_ref[...] = a * _sigmoid(gate)


def _glu_proj(xb, w, bias, tm, tn):
    n, d = xb.shape
    c = w.shape[1] // 2
    nc = c // tn
    return pl.pallas_call(
        _glu_kernel,
        grid=(n // tm, nc),
        in_specs=[pl.BlockSpec((tm, d), lambda i, j: (i, 0)),
                  pl.BlockSpec((d, tn), lambda i, j: (0, j)),
                  pl.BlockSpec((d, tn), lambda i, j: (0, nc + j)),
                  pl.BlockSpec((1, tn), lambda i, j: (0, j)),
                  pl.BlockSpec((1, tn), lambda i, j: (0, nc + j))],
        out_specs=pl.BlockSpec((tm, tn), lambda i, j: (i, j)),
        out_shape=jax.ShapeDtypeStruct((n, c), F32),
        compiler_params=_cparams(("parallel", "arbitrary")),
    )(xb, w, w, bias, bias)


def _dwconv_kernel(cur_ref, prev_ref, w_ref, bdw_ref, g_ref, b_ref, o_ref, buf_ref, sh_ref, y_ref,
                   *, ts, halo):
    i = pl.program_id(1)
    prev = prev_ref[...]
    buf_ref[0:halo, :] = jnp.where(i == 0, jnp.zeros_like(prev), prev)
    buf_ref[halo:halo + ts, :] = cur_ref[...]
    base = halo - (CONV_WIDTH - 1)
    span = sh_ref.shape[1]
    for cb in range(cur_ref.shape[1] // LANES):
        cols = slice(cb * LANES, (cb + 1) * LANES)
        for r in range(1, 8):
            sh_ref[r - 1] = buf_ref[r:r + span, cols]
        acc = jnp.zeros((ts, LANES), F32)
        for j in range(CONV_WIDTH):
            a, r = divmod(base + j, 8)
            if r == 0:
                win = buf_ref[8 * a:8 * a + ts, cols]
            else:
                win = sh_ref[r - 1, 8 * a:8 * a + ts, :]
            acc = acc + w_ref[j:j + 1, cols] * win
        y_ref[:, cols] = acc
    y = _layer_norm(y_ref[...] + bdw_ref[...], g_ref[...], b_ref[...])
    o_ref[...] = (y * _sigmoid(y)).astype(o_ref.dtype)


def _dwconv_ln_silu(h, w_dw, b_dw, g, b, batch, seq, ts):
    n, c = h.shape
    halo = 32
    nt = seq // ts
    ratio = ts // halo
    fixed = lambda bi, i: (0, 0)
    cur = lambda bi, i: (bi * nt + i, 0)
    prev = lambda bi, i: (jnp.maximum((bi * nt + i) * ratio - 1, 0), 0)
    return pl.pallas_call(
        functools.partial(_dwconv_kernel, ts=ts, halo=halo),
        grid=(batch, nt),
        in_specs=[pl.BlockSpec((ts, c), cur), pl.BlockSpec((halo, c), prev),
                  pl.BlockSpec((CONV_WIDTH, c), fixed),
                  pl.BlockSpec((1, c), fixed), pl.BlockSpec((1, c), fixed),
                  pl.BlockSpec((1, c), fixed)],
        out_specs=pl.BlockSpec((ts, c), cur),
        out_shape=jax.ShapeDtypeStruct((n, c), BF16),
        scratch_shapes=[pltpu.VMEM((halo + ts, c), F32),
                        pltpu.VMEM((7, ts + halo - 8, LANES), F32),
                        pltpu.VMEM((ts, c), F32)],
        compiler_params=_cparams(("parallel", "arbitrary")),
    )(h, h, w_dw, b_dw, g, b)


MOE_TILE = 512
MOE_EXPERT_TILE = 1024
NOT_ROUTED = -1e9


def _router_kernel(x_ref, w_ref, comb_ref, rank_ref, cum_ref, count_scr):
    @pl.when(pl.program_id(0) == 0)
    def _():
        count_scr[...] = jnp.zeros_like(count_scr)

    logits = jnp.dot(x_ref[...], w_ref[...], precision=lax.Precision.HIGHEST,
                     preferred_element_type=F32)
    tm = logits.shape[0]
    lane = lax.broadcasted_iota(jnp.int32, logits.shape, 1)
    logits = jnp.where(lane < N_EXPERTS, logits, -jnp.inf)
    m1 = jnp.max(logits, axis=1, keepdims=True)
    i1 = jnp.min(jnp.where(logits == m1, lane, LANES), axis=1, keepdims=True)
    rest = jnp.where(lane == i1, -jnp.inf, logits)
    m2 = jnp.max(rest, axis=1, keepdims=True)
    i2 = jnp.min(jnp.where(rest == m2, lane, LANES), axis=1, keepdims=True)
    e2 = jnp.exp(m2 - m1)
    g1 = 1.0 / (1.0 + e2)
    g2 = e2 / (1.0 + e2)
    comb_ref[...] = jnp.where(lane == i1, g1, 0.0) + jnp.where(lane == i2, g2, 0.0)

    routed = jnp.where(lane == i1, 1.0, jnp.where(lane == i2, 1.0, 0.0))
    r = lax.broadcasted_iota(jnp.int32, (tm, tm), 0)
    c = lax.broadcasted_iota(jnp.int32, (tm, tm), 1)
    earlier = jnp.where(c < r, 1.0, 0.0).astype(BF16)
    before = jnp.dot(earlier, routed.astype(BF16), preferred_element_type=F32)
    start = count_scr[0:1, :]
    rank_ref[...] = jnp.where(routed > 0.0, before + start, NOT_ROUTED)
    after = start + before[tm - 1:tm, :] + routed[tm - 1:tm, :]
    count_scr[...] = jnp.broadcast_to(after, count_scr.shape)
    cum_ref[...] = jnp.broadcast_to(after, cum_ref.shape)


def _router(x, w_router_padded, tm):
    n, d = x.shape
    row = lambda i: (i, 0)
    return pl.pallas_call(
        _router_kernel,
        grid=(n // tm,),
        in_specs=[pl.BlockSpec((tm, d), row),
                  pl.BlockSpec((d, LANES), lambda i: (0, 0))],
        out_specs=[pl.BlockSpec((tm, LANES), row), pl.BlockSpec((tm, LANES), row),
                   pl.BlockSpec((8, LANES), row)],
        out_shape=[jax.ShapeDtypeStruct((n, LANES), F32), jax.ShapeDtypeStruct((n, LANES), F32),
                   jax.ShapeDtypeStruct((n // tm * 8, LANES), F32)],
        scratch_shapes=[pltpu.VMEM((8, LANES), F32)],
        compiler_params=_cparams(("arbitrary",)),
    )(x, w_router_padded)


def _dispatch_kernel(g_idx, t_idx, valid, e_of_g, slot_ref, x_ref, comb_ref, xg_ref, gate_ref,
                     acc_ref, gacc_ref):
    i = pl.program_id(0)
    g = g_idx[i]
    first = jnp.logical_or(i == 0, g_idx[jnp.maximum(i - 1, 0)] != g)
    last = jnp.logical_or(i == pl.num_programs(0) - 1,
                          g_idx[jnp.minimum(i + 1, pl.num_programs(0) - 1)] != g)

    @pl.when(first)
    def _():
        acc_ref[...] = jnp.zeros_like(acc_ref)
        gacc_ref[...] = jnp.zeros_like(gacc_ref)

    @pl.when(valid[i] > 0)
    def _():
        t = acc_ref.shape[0]
        e = e_of_g[g]
        sub = lax.broadcasted_iota(jnp.int32, (8, t), 0)
        slot_row = jnp.sum(jnp.where(sub == e, slot_ref[...], 0.0), axis=0, keepdims=True)
        want = (g * t + lax.broadcasted_iota(jnp.int32, (t, t), 0)).astype(F32)
        sel = jnp.where(slot_row == want, 1.0, 0.0).astype(BF16)
        acc_ref[...] += jnp.dot(sel, x_ref[...], preferred_element_type=F32)
        comb = comb_ref[...]
        hi = comb.astype(BF16)
        lo = (comb - hi.astype(F32)).astype(BF16)
        gacc_ref[...] += (jnp.dot(sel, hi, preferred_element_type=F32)
                          + jnp.dot(sel, lo, preferred_element_type=F32))

    @pl.when(last)
    def _():
        xg_ref[...] = acc_ref[...].astype(xg_ref.dtype)
        gate_ref[...] = gacc_ref[...]


def _dispatch(work, e_of_g, slot_rows, xb, comb, n_slots):
    g_idx, t_idx, valid = work
    n, d = xb.shape
    t = MOE_TILE
    tok = lambda i, g, ti, v, e: (ti[i], 0)
    slot = lambda i, g, ti, v, e: (g[i], 0)
    return pl.pallas_call(
        _dispatch_kernel,
        grid_spec=pltpu.PrefetchScalarGridSpec(
            num_scalar_prefetch=4,
            grid=(g_idx.shape[0],),
            in_specs=[pl.BlockSpec((8, t), lambda i, g, ti, v, e: (0, ti[i])),
                      pl.BlockSpec((t, d), tok), pl.BlockSpec((t, LANES), tok)],
            out_specs=[pl.BlockSpec((t, d), slot), pl.BlockSpec((t, LANES), slot)],
            scratch_shapes=[pltpu.VMEM((t, d), F32), pltpu.VMEM((t, LANES), F32)]),
        out_shape=[jax.ShapeDtypeStruct((n_slots, d), BF16),
                   jax.ShapeDtypeStruct((n_slots, LANES), F32)],
        compiler_params=_cparams(("arbitrary",)),
    )(g_idx, t_idx, valid, e_of_g, slot_rows, xb, comb)


def _expert_kernel(e_of_big, nvalid, xg_ref, gate_ref, wg_ref, wu_ref, wd_ref, y_ref, acc_ref):
    b = pl.program_id(0)
    f = pl.program_id(1)
    nv = nvalid[b]

    @pl.when(nv > 0)
    def _():
        @pl.when(f == 0)
        def _():
            acc_ref[...] = jnp.zeros_like(acc_ref)

        tm = acc_ref.shape[0]
        e = e_of_big[b]
        live = lax.broadcasted_iota(jnp.int32, (tm, 1), 0) < nv
        xg = xg_ref[...]
        xg = jnp.where(live, xg, jnp.zeros_like(xg))
        gates = gate_ref[...]
        lane = lax.broadcasted_iota(jnp.int32, gates.shape, 1)
        ge = jnp.sum(jnp.where(jnp.logical_and(lane == e, live), gates, 0.0),
                     axis=1, keepdims=True)
        hg = jnp.dot(xg, wg_ref[...], preferred_element_type=F32)
        hu = jnp.dot(xg, wu_ref[...], preferred_element_type=F32)
        h = hg * _sigmoid(hg) * hu * ge
        acc_ref[...] += jnp.dot(h.astype(BF16), wd_ref[...], preferred_element_type=F32)

        @pl.when(f == pl.num_programs(1) - 1)
        def _():
            y_ref[...] = acc_ref[...].astype(y_ref.dtype)


def _expert_ffn(e_of_big, nvalid, xg, gate, wg, wu, wd, tf):
    n_slots, d = xg.shape
    edim = wg.shape[2]
    tm = MOE_EXPERT_TILE
    row = lambda b, f, e, nv: (b, 0)
    return pl.pallas_call(
        _expert_kernel,
        grid_spec=pltpu.PrefetchScalarGridSpec(
            num_scalar_prefetch=2,
            grid=(n_slots // tm, edim // tf),
            in_specs=[pl.BlockSpec((tm, d), row), pl.BlockSpec((tm, LANES), row),
                      pl.BlockSpec((None, d, tf), lambda b, f, e, nv: (e[b], 0, f)),
                      pl.BlockSpec((None, d, tf), lambda b, f, e, nv: (e[b], 0, f)),
                      pl.BlockSpec((None, tf, d), lambda b, f, e, nv: (e[b], f, 0))],
            out_specs=pl.BlockSpec((tm, d), row),
            scratch_shapes=[pltpu.VMEM((tm, d), F32)]),
        out_shape=jax.ShapeDtypeStruct((n_slots, d), BF16),
        compiler_params=_cparams(("parallel", "arbitrary")),
    )(e_of_big, nvalid, xg, gate, wg, wu, wd)


def _combine_kernel(t_idx, g_idx, valid, e_of_g, slot_ref, y_ref, x_ref, g_ref, b_ref,
                    o_ref, ob_ref, acc_ref):
    i = pl.program_id(0)
    ti = t_idx[i]
    first = jnp.logical_or(i == 0, t_idx[jnp.maximum(i - 1, 0)] != ti)
    last = jnp.logical_or(i == pl.num_programs(0) - 1,
                          t_idx[jnp.minimum(i + 1, pl.num_programs(0) - 1)] != ti)

    @pl.when(first)
    def _():
        acc_ref[...] = ALPHA * x_ref[...]

    @pl.when(valid[i] > 0)
    def _():
        t = acc_ref.shape[0]
        g = g_idx[i]
        e = e_of_g[g]
        slots = slot_ref[...]
        lane = lax.broadcasted_iota(jnp.int32, slots.shape, 1)
        slot_col = jnp.sum(jnp.where(lane == e, slots, 0.0), axis=1, keepdims=True)
        want = (g * t + lax.broadcasted_iota(jnp.int32, (t, t), 1)).astype(F32)
        sel = jnp.where(slot_col == want, 1.0, 0.0).astype(BF16)
        acc_ref[...] += jnp.dot(sel, y_ref[...], preferred_element_type=F32)

    @pl.when(last)
    def _():
        y = _layer_norm(acc_ref[...], g_ref[...], b_ref[...])
        o_ref[...] = y
        ob_ref[...] = y.astype(BF16)


def _combine_ln(work, e_of_g, slot_cols, y, x, g, b):
    t_idx, g_idx, valid = work
    n, d = x.shape
    t = MOE_TILE
    tok = lambda i, ti, gi, v, e: (ti[i], 0)
    fixed = lambda i, ti, gi, v, e: (0, 0)
    return pl.pallas_call(
        _combine_kernel,
        grid_spec=pltpu.PrefetchScalarGridSpec(
            num_scalar_prefetch=4,
            grid=(t_idx.shape[0],),
            in_specs=[pl.BlockSpec((t, LANES), tok),
                      pl.BlockSpec((t, d), lambda i, ti, gi, v, e: (gi[i], 0)),
                      pl.BlockSpec((t, d), tok),
                      pl.BlockSpec((1, d), fixed), pl.BlockSpec((1, d), fixed)],
            out_specs=[pl.BlockSpec((t, d), tok), pl.BlockSpec((t, d), tok)],
            scratch_shapes=[pltpu.VMEM((t, d), F32)]),
        out_shape=[jax.ShapeDtypeStruct((n, d), F32), jax.ShapeDtypeStruct((n, d), BF16)],
        compiler_params=_cparams(("arbitrary",)),
    )(t_idx, g_idx, valid, e_of_g, slot_cols, y, x, g, b)


def _routing_tables(rank, cum, n):
    t, te = MOE_TILE, MOE_EXPERT_TILE
    nt = n // t
    i32 = jnp.int32
    after = cum.reshape(nt, 8, LANES)[:, 0, :N_EXPERTS].astype(i32)
    before = jnp.concatenate([jnp.zeros((1, N_EXPERTS), i32), after[:-1]], axis=0)
    totals = after[-1]
    padded = (totals + te - 1) // te * te
    off = jnp.cumsum(padded) - padded
    n_slots = 2 * n + N_EXPERTS * te
    n_big = n_slots // te
    n_g = n_slots // t

    big_start = jnp.arange(n_big, dtype=i32) * te
    e_of_big = jnp.minimum(jnp.sum(big_start[:, None] >= (off + padded)[None, :], axis=1),
                           N_EXPERTS - 1).astype(i32)
    nvalid = jnp.clip(totals[e_of_big] - (big_start - off[e_of_big]), 0, te).astype(i32)
    e_of_g = jnp.repeat(e_of_big, te // t)

    off_row = jnp.concatenate([off.astype(F32), jnp.zeros((LANES - N_EXPERTS,), F32)])[None, :]
    slot_cols = rank + off_row
    slot_rows = jnp.transpose(slot_cols[:, :N_EXPERTS])

    has = after > before
    g0 = (off[None, :] + before) // t
    g1 = (off[None, :] + jnp.maximum(after, 1) - 1) // t
    t_ids = jnp.broadcast_to(jnp.arange(nt, dtype=i32)[:, None], (nt, N_EXPERTS))
    cand_g = jnp.stack([g0, g0 + 1], axis=-1).reshape(-1).astype(i32)
    cand_t = jnp.stack([t_ids, t_ids], axis=-1).reshape(-1)
    cand_ok = jnp.stack([has, jnp.logical_and(has, g1 > g0)], axis=-1).reshape(-1)
    n_items = min(N_EXPERTS * nt + n_g, 2 * N_EXPERTS * nt)
    count = jnp.sum(cand_ok.astype(i32))
    big = jnp.int32(2 ** 30)

    def ordered(major, minor, minor_range):
        order = jnp.argsort(jnp.where(cand_ok, major * minor_range + minor, big))[:n_items]
        pos = jnp.minimum(jnp.arange(n_items, dtype=i32), count - 1)
        pick = order[pos]
        ok = (jnp.arange(n_items, dtype=i32) < count).astype(i32)
        return cand_g[pick], cand_t[pick], ok

    dg, dt, dok = ordered(cand_g, cand_t, nt)
    cg, ct, cok = ordered(cand_t, cand_g, n_g)
    return dict(n_slots=n_slots, e_of_big=e_of_big, nvalid=nvalid, e_of_g=e_of_g,
                slot_cols=slot_cols, slot_rows=slot_rows,
                dispatch=(dg, dt, dok), combine=(ct, cg, cok))


def _moe_ln(xb, x, w_router_padded, wg, wu, wd, g, b, tf):
    n, _ = x.shape
    comb, rank, cum = _router(x, w_router_padded, MOE_TILE)
    rt = _routing_tables(rank, cum, n)
    xg, gate = _dispatch(rt["dispatch"], rt["e_of_g"], rt["slot_rows"], xb, comb, rt["n_slots"])
    y = _expert_ffn(rt["e_of_big"], rt["nvalid"], xg, gate, wg, wu, wd, tf)
    return _combine_ln(rt["combine"], rt["e_of_g"], rt["slot_cols"], y, x, g, b)


def _rope_patterns(seq):
    half = ROPE_DIM // 2
    inv_freq = ROPE_THETA ** (-jnp.arange(0, ROPE_DIM, 2, dtype=F32) / ROPE_DIM)
    ang = jnp.arange(seq, dtype=F32)[:, None] * inv_freq[None, :]
    cos, sin = jnp.cos(ang), jnp.sin(ang)
    rest = HEAD_DIM - ROPE_DIM
    c = jnp.concatenate([cos, cos, jnp.ones((seq, rest), F32)], axis=1)
    s1 = jnp.concatenate([-sin, jnp.zeros((seq, half + rest), F32)], axis=1)
    s2 = jnp.concatenate([jnp.zeros((seq, half), F32), sin, jnp.zeros((seq, rest), F32)], axis=1)
    return c, s1, s2


def _row_tile(n, want):
    return want if n % want == 0 else n


def _attention_layer(x, xb, w_in, b_f, w_out, g, b, tabs, batch, seq):
    n = batch * seq
    o = [0, 512, 1024, 1536, 1544, 2056, 2568, 3080, 3336, 3400, 3404]
    part = lambda k: w_in[:, o[k]:o[k + 1]]
    scale = HEAD_DIM ** -0.5
    pad = jnp.zeros((D_MODEL, LANES - N_FOX_HEADS - N_IDX_HEADS), F32)
    w_all = jnp.concatenate([part(0) * scale, part(1), part(2), part(4) * scale, part(5),
                             part(6), part(7), part(8), part(8), part(3), part(9), pad],
                            axis=1).astype(BF16)
    main, small = _project(xb, w_all, tabs, seq, _row_tile(seq, 512))
    bias_row = jnp.concatenate([b_f.astype(F32), jnp.zeros((LANES - N_FOX_HEADS,), F32)])[None, :]
    fcum = _forget_cumsum(small, bias_row, batch, seq)
    tq = min(ATT_BLOCK, seq)
    frow = jnp.transpose(fcum.reshape(batch, seq, LANES)[:, :, :N_FOX_HEADS], (0, 2, 1))
    frow = frow.reshape(batch, N_FOX_HEADS, seq // tq, tq)
    fox = _fox_attention(main, fcum, frow, batch, seq)
    dsa = _dsa_attention(main, small, batch, seq)
    wo = w_out.astype(BF16)
    zero_bias = jnp.zeros((1, D_MODEL), F32)
    return _matmul_ln([fox, dsa], [wo[:FOX_W], wo[FOX_W:]], zero_bias, x, g, b,
                      _row_tile(n, 512))


def kernel(x, norm_mix_g, norm_mix_b, norm_ffn_g, norm_ffn_b, attn_w_in, attn_b_f, attn_w_out,
           ffn_w_gate, ffn_w_up, ffn_w_down, conv_w_in, conv_b_in, conv_w_dw, conv_b_dw,
           conv_ln_g, conv_ln_b, conv_w_out, conv_b_out, moe_w_router, moe_w_gate, moe_w_up,
           moe_w_down):
    batch, seq, d = x.shape
    n = batch * seq
    tabs = tuple(jnp.tile(p, (1, MAIN_TN // HEAD_DIM)) for p in _rope_patterns(seq))
    r2 = lambda v: v.astype(F32).reshape(1, -1)

    xf = x.reshape(n, d)
    xb = xf.astype(BF16)
    for layer in range(DEPTH):
        j = layer // 2
        gm, bm = r2(norm_mix_g[layer]), r2(norm_mix_b[layer])
        gf, bf = r2(norm_ffn_g[layer]), r2(norm_ffn_b[layer])
        if layer % 2 == 0:
            xf, xb = _attention_layer(xf, xb, attn_w_in[j], attn_b_f[j], attn_w_out[j], gm, bm,
                                      tabs, batch, seq)
            xf, xb = _swiglu_ln(xb, xf, ffn_w_gate[j].astype(BF16), ffn_w_up[j].astype(BF16),
                                ffn_w_down[j].astype(BF16), gf, bf, _row_tile(n, 512), 1408)
        else:
            h = _glu_proj(xb, conv_w_in[j].astype(BF16), r2(conv_b_in[j]), _row_tile(n, 1024), 512)
            hb = _dwconv_ln_silu(h, conv_w_dw[j].astype(F32), r2(conv_b_dw[j]),
                                 r2(conv_ln_g[j]), r2(conv_ln_b[j]), batch, seq,
                                 min(256, seq))
            xf, xb = _matmul_ln([hb], [conv_w_out[j].astype(BF16)], r2(conv_b_out[j]), xf, gm, bm,
                                _row_tile(n, 512))
            w_r = jnp.concatenate([moe_w_router[j].astype(F32),
                                   jnp.zeros((d, LANES - N_EXPERTS), F32)], axis=1)
            xf, xb = _moe_ln(xb, xf, w_r, moe_w_gate[j].astype(BF16), moe_w_up[j].astype(BF16),
                             moe_w_down[j].astype(BF16), gf, bf, 512)
    return xf.reshape(batch, seq, d)
```
